```python
import numpy as np
import jax
import jax.numpy as jnp
from jax import lax

D_MODEL = 2048
BATCH = 4
SEQ = 4096
DEPTH = 2

HEAD_DIM = 128
N_HEADS = D_MODEL // HEAD_DIM
N_HEADS_FOX = N_HEADS // 2
N_HEADS_DIL = N_HEADS - N_HEADS_FOX
DILATED_PATTERNS = ((128, 1), (512, 4), (2048, 16))
N_HEADS_NSA = N_HEADS
N_KV_NSA = 4
CMP_BLOCK = 32
CMP_STRIDE = 16
SLC_BLOCK = 64
N_SELECT = 16
WINDOW_NSA = 512
Q_BLOCK = 128
NSA_Q_BLOCK = 32
ROPE_THETA = 500000.0
ROPE_DIM = HEAD_DIM // 4
D_FF = 4 * D_MODEL
EPS = 1e-6
NEG = -1e30
FORCED_SCORE = 1e9
SCALE = HEAD_DIM ** -0.5
N_EVEN = (DEPTH + 1) // 2
N_ODD = DEPTH // 2

FOX_W = N_HEADS_FOX * HEAD_DIM
DIL_W = N_HEADS_DIL * HEAD_DIM
EVEN_IN = 4 * FOX_W + N_HEADS_FOX + 3 * DIL_W
Q_NSA_W = N_HEADS_NSA * HEAD_DIM
KV_NSA_W = N_KV_NSA * HEAD_DIM
ODD_IN = Q_NSA_W + 6 * KV_NSA_W + 3 * N_HEADS_NSA

kernel_name = 'hybrid_fox_dilated_nsa_trunk'


def rms_norm(x, g):
    xf = x.astype(jnp.float32)
    y = xf * lax.rsqrt(jnp.mean(xf * xf, axis=-1, keepdims=True) + EPS)
    return (y * g.astype(jnp.float32)).astype(x.dtype)


def rope_tables(pos):
    inv = 1.0 / (ROPE_THETA ** (jnp.arange(0, ROPE_DIM, 2, dtype=jnp.float32) / ROPE_DIM))
    ang = pos.astype(jnp.float32)[..., None] * inv
    return jnp.cos(ang), jnp.sin(ang)


def partial_rope(x, cos, sin):
    half = ROPE_DIM // 2
    x1 = x[..., :half]
    x2 = x[..., half:ROPE_DIM]
    c = cos.astype(x.dtype)
    s = sin.astype(x.dtype)
    return jnp.concatenate([x1 * c - x2 * s, x2 * c + x1 * s, x[..., ROPE_DIM:]], axis=-1)


def masked_softmax(s, mask):
    s = jnp.where(mask, s.astype(jnp.float32), NEG)
    m = jnp.max(s, axis=-1, keepdims=True)
    e = jnp.where(mask, jnp.exp(s - m), 0.0)
    return e / jnp.maximum(jnp.sum(e, axis=-1, keepdims=True), 1e-30)


def sq_relu_mlp(h, w_up, w_down):
    return jnp.square(jax.nn.relu(h @ w_up)) @ w_down


def fox_attention(q, k, v, log_f):
    B, H, S, Dh = q.shape
    c = jnp.cumsum(log_f, axis=-1)
    nb = S // Q_BLOCK
    qb = jnp.moveaxis(q.reshape(B, H, nb, Q_BLOCK, Dh), 2, 0)
    cb = jnp.moveaxis(c.reshape(B, H, nb, Q_BLOCK), 2, 0)
    kpos = jnp.arange(S)

    def block(args):
        i, qi, ci = args
        s = jnp.einsum('bhqd,bhkd->bhqk', qi, k).astype(jnp.float32) * SCALE
        s = s + ci[..., :, None] - c[..., None, :]
        qpos = i * Q_BLOCK + jnp.arange(Q_BLOCK)
        p = masked_softmax(s, kpos[None, :] <= qpos[:, None])
        return jnp.einsum('bhqk,bhkd->bhqd', p, v)

    o = lax.map(block, (jnp.arange(nb), qb, cb))
    return jnp.moveaxis(o, 0, 2).reshape(B, H, S, Dh)


def dilated_pattern(q, k, v, dilation, steps):
    B, H, S, Dh = q.shape
    L = S // dilation
    qb = min(Q_BLOCK, L)
    nb = -(-L // qb)
    pad_r = nb * qb - L

    def to_sub(t):
        return t.reshape(B, H, L, dilation, Dh).transpose(0, 1, 3, 2, 4)

    pad_q = ((0, 0), (0, 0), (0, 0), (0, pad_r), (0, 0))
    pad_k = ((0, 0), (0, 0), (0, 0), (qb, pad_r), (0, 0))
    qs = jnp.pad(to_sub(q), pad_q).reshape(B, H, dilation, nb, qb, Dh)
    kp = jnp.pad(to_sub(k), pad_k).reshape(B, H, dilation, nb + 1, qb, Dh)
    vp = jnp.pad(to_sub(v), pad_k).reshape(B, H, dilation, nb + 1, qb, Dh)
    kw = jnp.concatenate([kp[:, :, :, :-1], kp[:, :, :, 1:]], axis=4)
    vw = jnp.concatenate([vp[:, :, :, :-1], vp[:, :, :, 1:]], axis=4)
    s = jnp.einsum('bhrnqd,bhrnkd->bhrnqk', qs, kw).astype(jnp.float32) * SCALE
    qm = (jnp.arange(nb) * qb)[:, None] + jnp.arange(qb)[None, :]
    km = (jnp.arange(nb) * qb - qb)[:, None] + jnp.arange(2 * qb)[None, :]
    dist = qm[:, :, None] - km[:, None, :]
    mask = (dist >= 0) & (dist <= steps) & (km[:, None, :] >= 0)
    s = jnp.where(mask, s, NEG)
    m = jnp.max(s, axis=-1)
    e = jnp.where(mask, jnp.exp(s - m[..., None]), 0.0)
    den = jnp.sum(e, axis=-1)
    num = jnp.einsum('bhrnqk,bhrnkd->bhrnqd', e, vw.astype(jnp.float32))

    def from_sub(t):
        rest = t.shape[5:]
        t = t.reshape((B, H, dilation, nb * qb) + rest)[:, :, :, :L]
        t = jnp.moveaxis(t, 2, 3)
        return t.reshape((B, H, S) + rest)

    return from_sub(m), from_sub(den), from_sub(num)


def dilated_attention(q, k, v):
    parts = [dilated_pattern(q, k, v, d, w // d) for (w, d) in DILATED_PATTERNS]
    m_all = jnp.max(jnp.stack([p[0] for p in parts]), axis=0)
    den = None
    num = None
    for m, l, n in parts:
        w = jnp.exp(m - m_all)
        den = w * l if den is None else den + w * l
        num = w[..., None] * n if num is None else num + w[..., None] * n
    return num / den[..., None]


def even_mixer(h, w_in, b_f, w_out, g_q_fox, g_k_fox, g_q_dil, g_k_dil, cos, sin):
    B, S, _ = h.shape
    proj = h @ w_in
    splits = np.cumsum([FOX_W, FOX_W, FOX_W, FOX_W, N_HEADS_FOX, DIL_W, DIL_W]).tolist()
    qa, ka, va, ga, fa, qd, kd, vd = jnp.split(proj, splits, axis=-1)

    def heads(t, n):
        return t.reshape(B, S, n, HEAD_DIM)

    def bhsd(t):
        return t.transpose(0, 2, 1, 3)

    qa = rms_norm(heads(qa, N_HEADS_FOX), g_q_fox)
    ka = rms_norm(heads(ka, N_HEADS_FOX), g_k_fox)
    log_f = jax.nn.log_sigmoid((fa + b_f).astype(jnp.float32))
    o_a = fox_attention(bhsd(qa), bhsd(ka), bhsd(heads(va, N_HEADS_FOX)), log_f.transpose(0, 2, 1))
    o_a = bhsd(o_a).astype(h.dtype) * jax.nn.sigmoid(heads(ga, N_HEADS_FOX))
    c3, s3 = cos[:, None, :], sin[:, None, :]
    qd = partial_rope(rms_norm(heads(qd, N_HEADS_DIL), g_q_dil), c3, s3)
    kd = partial_rope(rms_norm(heads(kd, N_HEADS_DIL), g_k_dil), c3, s3)
    o_b = dilated_attention(bhsd(qd), bhsd(kd), bhsd(heads(vd, N_HEADS_DIL)))
    o_b = bhsd(o_b).astype(h.dtype)
    o = jnp.concatenate([o_a.reshape(B, S, FOX_W), o_b.reshape(B, S, DIL_W)], axis=-1)
    return o @ w_out


def compress(x_tok, pe, w1, w2):
    S = x_tok.shape[2]
    n_cmp = (S - CMP_BLOCK) // CMP_STRIDE + 1
    idx = np.arange(n_cmp)[:, None] * CMP_STRIDE + np.arange(CMP_BLOCK)[None, :]
    blk = x_tok[:, :, idx] + pe
    flat = blk.reshape(blk.shape[:3] + (CMP_BLOCK * HEAD_DIM,))
    return jax.nn.gelu(flat @ w1) @ w2


def cmp_slc_overlap(n_cmp, n_slc):
    start = np.arange(n_cmp) * CMP_STRIDE
    js = np.arange(n_slc) * SLC_BLOCK
    ov = (start[:, None] < js[None, :] + SLC_BLOCK) & (start[:, None] + CMP_BLOCK > js[None, :])
    return ov.astype(np.float32)


def nsa_attention(q, kc, vc, k_slc, v_slc, k_win, v_win, gates):
    B, G, HPG, S, Dh = q.shape
    n_cmp = kc.shape[2]
    n_slc = S // SLC_BLOCK
    n_sel = min(N_SELECT, n_slc)
    qb = NSA_Q_BLOCK
    nb = S // qb
    cmp_end = jnp.arange(n_cmp) * CMP_STRIDE + CMP_BLOCK - 1
    overlap = jnp.asarray(cmp_slc_overlap(n_cmp, n_slc))
    ks_blocks = k_slc.reshape(B, G, n_slc, SLC_BLOCK, Dh)
    vs_blocks = v_slc.reshape(B, G, n_slc, SLC_BLOCK, Dh)
    kw_pad = jnp.pad(k_win, ((0, 0), (0, 0), (WINDOW_NSA, 0), (0, 0)))
    vw_pad = jnp.pad(v_win, ((0, 0), (0, 0), (WINDOW_NSA, 0), (0, 0)))
    b_ix = jnp.arange(B)[:, None, None, None]
    g_ix = jnp.arange(G)[None, :, None, None]
    blk_ix = jnp.arange(n_slc)
    q_st = jnp.moveaxis(q.reshape(B, G, HPG, nb, qb, Dh), 3, 0)
    g_st = jnp.moveaxis(gates.reshape(B, G, HPG, nb, qb, 3), 3, 0)

    def block(args):
        i, qi, gi = args
        qpos = i * qb + jnp.arange(qb)
        s_c = jnp.einsum('bghqd,bgnd->bghqn', qi, kc).astype(jnp.float32) * SCALE
        p_c = masked_softmax(s_c, cmp_end[None, :] <= qpos[:, None])
        o_c = jnp.einsum('bghqn,bgnd->bghqd', p_c, vc)
        imp = jnp.einsum('bghqn,nj->bgqj', p_c, overlap)
        cur = (qpos // SLC_BLOCK)[:, None]
        forced = (blk_ix == 0) | (blk_ix == cur) | (blk_ix == cur - 1)
        imp = jnp.where(forced, FORCED_SCORE, imp)
        imp = jnp.where(blk_ix <= cur, imp, NEG)
        top_val, top_idx = lax.top_k(imp, n_sel)
        k_sel = ks_blocks[b_ix, g_ix, top_idx].reshape(B, G, qb, n_sel * SLC_BLOCK, Dh)
        v_sel = vs_blocks[b_ix, g_ix, top_idx].reshape(B, G, qb, n_sel * SLC_BLOCK, Dh)
        tok = (top_idx[..., None] * SLC_BLOCK + jnp.arange(SLC_BLOCK)).reshape(B, G, qb, n_sel * SLC_BLOCK)
        ok = jnp.repeat(top_val > 0.5 * NEG, SLC_BLOCK, axis=-1) & (tok <= qpos[:, None])
        s_s = jnp.einsum('bghqd,bgqkd->bghqk', qi, k_sel).astype(jnp.float32) * SCALE
        p_s = masked_softmax(s_s, ok[:, :, None])
        o_s = jnp.einsum('bghqk,bgqkd->bghqd', p_s, v_sel)
        kw = lax.dynamic_slice_in_dim(kw_pad, i * qb, qb + WINDOW_NSA, axis=2)
        vw = lax.dynamic_slice_in_dim(vw_pad, i * qb, qb + WINDOW_NSA, axis=2)
        kpos = i * qb - WINDOW_NSA + jnp.arange(qb + WINDOW_NSA)
        dist = qpos[:, None] - kpos[None, :]
        mask_w = (dist >= 0) & (dist < WINDOW_NSA) & (kpos[None, :] >= 0)
        s_w = jnp.einsum('bghqd,bgkd->bghqk', qi, kw).astype(jnp.float32) * SCALE
        p_w = masked_softmax(s_w, mask_w)
        o_w = jnp.einsum('bghqk,bgkd->bghqd', p_w, vw)
        gi = gi.astype(jnp.float32)
        return gi[..., 0:1] * o_c + gi[..., 1:2] * o_s + gi[..., 2:3] * o_w

    o = lax.map(block, (jnp.arange(nb), q_st, g_st))
    return jnp.moveaxis(o, 0, 3).reshape(B, G, HPG, S, Dh)


def odd_mixer(h, w_in, w_out, k_pe, k_w1, k_w2, v_pe, v_w1, v_w2, g_q, g_kc, g_ks, g_kw, cos, sin):
    B, S, _ = h.shape
    G, HPG = N_KV_NSA, N_HEADS_NSA // N_KV_NSA
    proj = h @ w_in
    splits = np.cumsum([Q_NSA_W] + [KV_NSA_W] * 6).tolist()
    q, kc, vc, ks, vs, kw, vw, gt = jnp.split(proj, splits, axis=-1)
    c3, s3 = cos[:, None, :], sin[:, None, :]
    q = partial_rope(rms_norm(q.reshape(B, S, N_HEADS_NSA, HEAD_DIM), g_q), c3, s3)
    q = q.reshape(B, S, G, HPG, HEAD_DIM).transpose(0, 2, 3, 1, 4)

    def kv(t):
        return t.reshape(B, S, G, HEAD_DIM)

    def to_bg(t):
        return t.transpose(0, 2, 1, 3)

    ks = to_bg(partial_rope(rms_norm(kv(ks), g_ks), c3, s3))
    kw = to_bg(partial_rope(rms_norm(kv(kw), g_kw), c3, s3))
    vs = to_bg(kv(vs))
    vw = to_bg(kv(vw))
    n_cmp = (S - CMP_BLOCK) // CMP_STRIDE + 1
    cos_c, sin_c = rope_tables(jnp.arange(n_cmp) * CMP_STRIDE + CMP_BLOCK - 1)
    kc = partial_rope(rms_norm(compress(to_bg(kv(kc)), k_pe, k_w1, k_w2), g_kc), cos_c, sin_c)
    vc = compress(to_bg(kv(vc)), v_pe, v_w1, v_w2)
    gates = jax.nn.sigmoid(gt.reshape(B, S, G, HPG, 3)).transpose(0, 2, 3, 1, 4)
    o = nsa_attention(q, kc, vc, ks, vs, kw, vw, gates)
    o = o.transpose(0, 3, 1, 2, 4).reshape(B, S, Q_NSA_W).astype(h.dtype)
    return o @ w_out


def setup_inputs(seed: int = 0) -> dict:
    key = jax.random.key(seed)
    keys = iter(jax.random.split(key, 40))

    def nrm(shape, scale):
        return jax.random.normal(next(keys), shape, jnp.float32) * scale

    def gain(shape):
        return 1.0 + 0.02 * jax.random.normal(next(keys), shape, jnp.float32)

    D = D_MODEL
    return {
        'x': nrm((BATCH, SEQ, D), 1.0),
        'ln_mix_g': gain((DEPTH, D)),
        'ln_mlp_g': gain((DEPTH, D)),
        'w_mlp_up': nrm((DEPTH, D, D_FF), D ** -0.5),
        'w_mlp_down': nrm((DEPTH, D_FF, D), D_FF ** -0.5),
        'even_w_in': nrm((N_EVEN, D, EVEN_IN), D ** -0.5),
        'even_b_f': nrm((N_EVEN, N_HEADS_FOX), 0.1),
        'even_w_out': nrm((N_EVEN, FOX_W + DIL_W, D), (FOX_W + DIL_W) ** -0.5),
        'even_g_q_fox': gain((N_EVEN, HEAD_DIM)),
        'even_g_k_fox': gain((N_EVEN, HEAD_DIM)),
        'even_g_q_dil': gain((N_EVEN, HEAD_DIM)),
        'even_g_k_dil': gain((N_EVEN, HEAD_DIM)),
        'odd_w_in': nrm((N_ODD, D, ODD_IN), D ** -0.5),
        'odd_w_out': nrm((N_ODD, Q_NSA_W, D), Q_NSA_W ** -0.5),
        'odd_phi_k_pe': nrm((N_ODD, CMP_BLOCK, HEAD_DIM), 0.1),
        'odd_phi_k_w1': nrm((N_ODD, CMP_BLOCK * HEAD_DIM, HEAD_DIM), (CMP_BLOCK * HEAD_DIM) ** -0.5),
        'odd_phi_k_w2': nrm((N_ODD, HEAD_DIM, HEAD_DIM), HEAD_DIM ** -0.5),
        'odd_phi_v_pe': nrm((N_ODD, CMP_BLOCK, HEAD_DIM), 0.1),
        'odd_phi_v_w1': nrm((N_ODD, CMP_BLOCK * HEAD_DIM, HEAD_DIM), (CMP_BLOCK * HEAD_DIM) ** -0.5),
        'odd_phi_v_w2': nrm((N_ODD, HEAD_DIM, HEAD_DIM), HEAD_DIM ** -0.5),
        'odd_g_q': gain((N_ODD, HEAD_DIM)),
        'odd_g_kc': gain((N_ODD, HEAD_DIM)),
        'odd_g_ks': gain((N_ODD, HEAD_DIM)),
        'odd_g_kw': gain((N_ODD, HEAD_DIM)),
    }


def reference(x, ln_mix_g, ln_mlp_g, w_mlp_up, w_mlp_down, even_w_in, even_b_f, even_w_out,
              even_g_q_fox, even_g_k_fox, even_g_q_dil, even_g_k_dil, odd_w_in, odd_w_out,
              odd_phi_k_pe, odd_phi_k_w1, odd_phi_k_w2, odd_phi_v_pe, odd_phi_v_w1, odd_phi_v_w2,
              odd_g_q, odd_g_kc, odd_g_ks, odd_g_kw):
    S = x.shape[1]
    cos, sin = rope_tables(jnp.arange(S))
    h = x
    for layer in range(DEPTH):
        hn = rms_norm(h, ln_mix_g[layer])
        if layer % 2 == 0:
            i = layer // 2
            h = h + even_mixer(hn, even_w_in[i], even_b_f[i], even_w_out[i], even_g_q_fox[i],
                               even_g_k_fox[i], even_g_q_dil[i], even_g_k_dil[i], cos, sin)
        else:
            i = layer // 2
            h = h + odd_mixer(hn, odd_w_in[i], odd_w_out[i], odd_phi_k_pe[i], odd_phi_k_w1[i],
                              odd_phi_k_w2[i], odd_phi_v_pe[i], odd_phi_v_w1[i], odd_phi_v_w2[i],
                              odd_g_q[i], odd_g_kc[i], odd_g_ks[i], odd_g_kw[i], cos, sin)
        hn = rms_norm(h, ln_mlp_g[layer])
        h = h + sq_relu_mlp(hn, w_mlp_up[layer], w_mlp_down[layer])
    return h
```

```python
import functools
from typing import NamedTuple

import numpy as np
import jax
import jax.numpy as jnp
from jax import lax
from jax.experimental import pallas as pl
from jax.experimental.pallas import tpu as pltpu

HEAD_DIM = 128
LANES = 128
N_KV_NSA = 4
DILATED_PATTERNS = ((128, 1), (512, 4), (2048, 16))
CMP_BLOCK = 32
CMP_STRIDE = 16
SLC_BLOCK = 64
N_SELECT = 16
WINDOW_NSA = 512
ROPE_THETA = 500000.0
ROPE_DIM = HEAD_DIM // 4
EPS = 1e-6
NEG = -1e30
FORCED_SCORE = 1e9
SCALE = HEAD_DIM ** -0.5
VMEM_LIMIT = 56 * 1024 * 1024

BF16 = jnp.bfloat16
F32 = jnp.float32
_NT = (((1,), (1,)), ((), ()))


def _params(*sem):
    return pltpu.CompilerParams(dimension_semantics=sem, vmem_limit_bytes=VMEM_LIMIT)


def _split3(x):
    hi = x.astype(BF16)
    r1 = x - hi.astype(F32)
    mid = r1.astype(BF16)
    lo = (r1 - mid.astype(F32)).astype(BF16)
    return hi, mid, lo


class Seg(NamedTuple):
    width: int
    norm: bool = False
    gain: int = 0
    rope: bool = False
    scale: float = 1.0
    sigmoid: bool = False


def _rope(y, tab_ref):
    return (y * tab_ref[0] + pltpu.roll(y, LANES - ROPE_DIM // 2, 1) * tab_ref[1]
            + pltpu.roll(y, ROPE_DIM // 2, 1) * tab_ref[2])


def _head_epilogue(y, seg, gains_ref, tab_ref):
    if seg.norm:
        ms = jnp.mean(y * y, axis=-1, keepdims=True)
        y = y * lax.rsqrt(ms + EPS) * gains_ref[seg.gain:seg.gain + 1, :]
    if seg.rope:
        y = _rope(y, tab_ref)
    if seg.scale != 1.0:
        y = y * seg.scale
    if seg.sigmoid:
        y = 1.0 / (1.0 + jnp.exp(-y))
    return y


def _proj_kernel(x_ref, g_ref, w_ref, gains_ref, tab_ref, o_ref, xn_ref, *, segs, tn):
    j = pl.program_id(1)

    @pl.when(j == 0)
    def _():
        x = x_ref[...]
        ms = jnp.mean(x * x, axis=-1, keepdims=True)
        xn_ref[...] = (x * lax.rsqrt(ms + EPS) * g_ref[...]).astype(BF16)

    acc = jnp.dot(xn_ref[...], w_ref[...], preferred_element_type=F32)
    lo = 0
    for seg in segs:
        nt = seg.width // tn

        @pl.when((j >= lo) & (j < lo + nt))
        def _(seg=seg):
            for hd in range(tn // LANES):
                sl = slice(hd * LANES, (hd + 1) * LANES)
                o_ref[:, sl] = _head_epilogue(acc[:, sl], seg, gains_ref, tab_ref).astype(o_ref.dtype)

        lo += nt


def _norm_proj(x, g, w, gains, tabs, segs, *, seq, tm, tn, out_dtype):
    T, D = x.shape
    N = w.shape[1]
    assert sum(s.width for s in segs) == N and all(s.width % tn == 0 for s in segs)
    assert T % tm == 0 and seq % tm == 0 and tn % LANES == 0
    nseq = seq // tm
    return pl.pallas_call(
        functools.partial(_proj_kernel, segs=tuple(segs), tn=tn),
        grid=(T // tm, N // tn),
        in_specs=[
            pl.BlockSpec((tm, D), lambda i, j: (i, 0)),
            pl.BlockSpec((1, D), lambda i, j: (0, 0)),
            pl.BlockSpec((D, tn), lambda i, j: (0, j)),
            pl.BlockSpec(gains.shape, lambda i, j: (0, 0)),
            pl.BlockSpec((3, tm, LANES), lambda i, j: (0, i % nseq, 0)),
        ],
        out_specs=pl.BlockSpec((tm, tn), lambda i, j: (i, j)),
        out_shape=jax.ShapeDtypeStruct((T, N), out_dtype),
        scratch_shapes=[pltpu.VMEM((tm, D), BF16)],
        compiler_params=_params("parallel", "arbitrary"),
        name="norm_proj",
    )(x, g.reshape(1, D), w, gains, tabs)


def _outproj_kernel(*refs, n_parts):
    a_refs = refs[:n_parts]
    w_refs = refs[n_parts:2 * n_parts]
    h_ref = refs[2 * n_parts]
    o_ref = refs[2 * n_parts + 1]
    acc = h_ref[...]
    for a_ref, w_ref in zip(a_refs, w_refs):
        acc = acc + jnp.dot(a_ref[...], w_ref[...], preferred_element_type=F32)
    o_ref[...] = acc


def _out_proj(parts, weights, h, *, tm, tn):
    T, D = h.shape
    n = len(parts)
    in_specs = [pl.BlockSpec((tm, p.shape[1]), lambda i, j: (i, 0)) for p in parts]
    in_specs += [pl.BlockSpec((w.shape[0], tn), lambda i, j: (0, j)) for w in weights]
    in_specs += [pl.BlockSpec((tm, tn), lambda i, j: (i, j))]
    return pl.pallas_call(
        functools.partial(_outproj_kernel, n_parts=n),
        grid=(T // tm, D // tn),
        in_specs=in_specs,
        out_specs=pl.BlockSpec((tm, tn), lambda i, j: (i, j)),
        out_shape=jax.ShapeDtypeStruct((T, D), F32),
        compiler_params=_params("parallel", "arbitrary"),
        name="out_proj",
    )(*parts, *weights, h)


def _mlp_kernel(h_ref, g_ref, wu_ref, wd_ref, o_ref, xn_ref):
    f = pl.program_id(1)

    @pl.when(f == 0)
    def _():
        x = h_ref[...]
        ms = jnp.mean(x * x, axis=-1, keepdims=True)
        xn_ref[...] = (x * lax.rsqrt(ms + EPS) * g_ref[...]).astype(BF16)
        o_ref[...] = x

    u = jnp.dot(xn_ref[...], wu_ref[...], preferred_element_type=F32)
    a = jnp.square(jnp.maximum(u, 0.0)).astype(BF16)
    o_ref[...] += jnp.dot(a, wd_ref[...], preferred_element_type=F32)


def _mlp(h, g, w_up, w_down, *, tm, tf):
    T, D = h.shape
    FF = w_up.shape[1]
    return pl.pallas_call(
        _mlp_kernel,
        grid=(T // tm, FF // tf),
        in_specs=[
            pl.BlockSpec((tm, D), lambda i, f: (i, 0)),
            pl.BlockSpec((1, D), lambda i, f: (0, 0)),
            pl.BlockSpec((D, tf), lambda i, f: (0, f)),
            pl.BlockSpec((tf, D), lambda i, f: (f, 0)),
        ],
        out_specs=pl.BlockSpec((tm, D), lambda i, f: (i, 0)),
        out_shape=jax.ShapeDtypeStruct((T, D), F32),
        scratch_shapes=[pltpu.VMEM((tm, D), BF16)],
        compiler_params=_params("parallel", "arbitrary"),
        name="sq_relu_mlp",
    )(h, g.reshape(1, D), w_up, w_down)


def _foxaug_kernel(f_ref, bf_ref, tri_ref, pq_ref, pk_ref, oq_ref, ok_ref, carry_ref, *, ts):
    @pl.when(pl.program_id(1) == 0)
    def _():
        carry_ref[...] = jnp.zeros_like(carry_ref)

    x = f_ref[...] + bf_ref[...]
    logf = jnp.minimum(x, 0.0) - jnp.log(1.0 + jnp.exp(-jnp.abs(x)))
    tri = tri_ref[...]
    c = carry_ref[...]
    for part in _split3(logf):
        c = c + jnp.dot(tri, part, preferred_element_type=F32)
    carry_ref[...] = c[ts - 1:ts, :]
    parts = jnp.concatenate(list(_split3(c)) + [jnp.ones((ts, LANES), BF16)], axis=1)
    oq_ref[...] = jnp.dot(parts, pq_ref[...], preferred_element_type=F32).astype(BF16)
    ok_ref[...] = jnp.dot(parts, pk_ref[...], preferred_element_type=F32).astype(BF16)


def _fox_aug(fproj, b_f, *, batch, seq, hf, ts):
    T = fproj.shape[0]
    ns = seq // ts
    tri = np.tril(np.ones((ts, ts), np.float32))
    pq = np.zeros((4 * LANES, hf * LANES), np.float32)
    pk = np.zeros((4 * LANES, hf * LANES), np.float32)
    for h in range(hf):
        for p in range(3):
            pq[p * LANES + h, h * LANES + p] = 1.0
            pq[3 * LANES, h * LANES + 3 + p] = 1.0
            pk[3 * LANES, h * LANES + p] = 1.0
            pk[p * LANES + h, h * LANES + 3 + p] = -1.0
    bf = jnp.zeros((1, LANES), F32).at[0, :hf].set(b_f)
    out = jax.ShapeDtypeStruct((T, hf * LANES), BF16)
    return pl.pallas_call(
        functools.partial(_foxaug_kernel, ts=ts),
        grid=(batch, ns),
        in_specs=[
            pl.BlockSpec((ts, LANES), lambda b, s: (b * ns + s, 0)),
            pl.BlockSpec((1, LANES), lambda b, s: (0, 0)),
            pl.BlockSpec((ts, ts), lambda b, s: (0, 0)),
            pl.BlockSpec(pq.shape, lambda b, s: (0, 0)),
            pl.BlockSpec(pk.shape, lambda b, s: (0, 0)),
        ],
        out_specs=[pl.BlockSpec((ts, hf * LANES), lambda b, s: (b * ns + s, 0))] * 2,
        out_shape=[out, out],
        scratch_shapes=[pltpu.VMEM((1, LANES), F32)],
        compiler_params=_params("parallel", "arbitrary"),
        name="fox_aug",
    )(fproj, bf, jnp.asarray(tri, BF16), jnp.asarray(pq, BF16), jnp.asarray(pk, BF16))


def _flash_kernel(*refs, hs, t, has_qaug, has_kaug, n_off, full_causal, gate_mode, gate_base,
                  has_addin):
    it = iter(refs)
    q_ref = next(it)
    qa_ref = next(it) if has_qaug else None
    k_ref = next(it)
    ka_ref = next(it) if has_kaug else None
    v_ref = next(it)
    bias_ref = next(it)
    gate_ref = next(it) if gate_mode else None
    add_ref = next(it) if has_addin else None
    o_ref = next(it)
    m_scr, l_scr, acc_scr = next(it), next(it), next(it)

    qi = pl.program_id(2)
    M = hs * t

    def q_head(h):
        qh = q_ref[:, h * LANES:(h + 1) * LANES]
        return jnp.concatenate([qh, qa_ref[...]], axis=1) if has_qaug else qh

    q = jnp.concatenate([q_head(h) for h in range(hs)], axis=0) if hs > 1 else q_head(0)

    m_scr[...] = jnp.full(m_scr.shape, NEG, F32)
    l_scr[...] = jnp.zeros(l_scr.shape, F32)
    acc_scr[...] = jnp.zeros(acc_scr.shape, F32)

    def step(kt, off):
        k0 = pl.multiple_of(kt * t, t)
        k = k_ref[pl.ds(k0, t), :]
        if has_kaug:
            k = jnp.concatenate([k, ka_ref[pl.ds(k0, t), :]], axis=1)
        s = lax.dot_general(q, k, _NT, preferred_element_type=F32)
        if off is not None:
            b = bias_ref[off]
            s = (s.reshape(hs, t, t) + b[None]).reshape(M, t) if hs > 1 else s + b
        m_prev = m_scr[...]
        m_new = jnp.maximum(m_prev, jnp.max(s, axis=1, keepdims=True))
        alpha = jnp.exp(m_prev - m_new)
        p = jnp.exp(s - m_new)
        l_scr[...] = alpha * l_scr[...] + jnp.sum(p, axis=1, keepdims=True)
        acc_scr[...] = alpha * acc_scr[...] + jnp.dot(
            p.astype(BF16), v_ref[pl.ds(k0, t), :], preferred_element_type=F32)
        m_scr[...] = m_new

    if full_causal:
        def body(kt, c):
            step(kt, None)
            return c
        lax.fori_loop(0, qi, body, 0)
        step(qi, 0)
    else:
        def body(kt, c):
            step(kt, qi - kt)
            return c
        lax.fori_loop(jnp.maximum(qi - (n_off - 1), 0), qi + 1, body, 0)

    o = acc_scr[...] * (1.0 / l_scr[...])
    for h in range(hs):
        oh = o[h * t:(h + 1) * t, :]
        if gate_mode == "elem":
            oh = oh * gate_ref[...].astype(F32)
        elif gate_mode == "col":
            idx = (pl.program_id(1) * hs + h) * 3 + gate_base
            lane = lax.broadcasted_iota(jnp.int32, (t, LANES), 1)
            oh = oh * jnp.sum(jnp.where(lane == idx, gate_ref[...], 0.0), axis=1, keepdims=True)
        sl = slice(h * LANES, (h + 1) * LANES)
        if has_addin:
            oh = oh + add_ref[:, sl]
        o_ref[:, sl] = oh.astype(o_ref.dtype)


def _flash(q_arr, q_col0, k_arr, k_col0, v_arr, v_col0, bias, *, n_kv, hs, t, out_dtype,
           qaug=None, kaug=None, full_causal=False, gate=None, gate_mode=None, gate_col0=0,
           gate_base=0, addin=None):
    B, S, _ = q_arr.shape
    qw = hs * LANES
    assert q_col0 % qw == 0 and k_col0 % LANES == 0 and v_col0 % LANES == 0 and S % t == 0
    qb, kb, vb = q_col0 // qw, k_col0 // LANES, v_col0 // LANES
    ins, specs = [q_arr], [pl.BlockSpec((None, t, qw), lambda b, h, i: (b, i, qb + h))]
    if qaug is not None:
        ins.append(qaug)
        if qaug.ndim == 3:
            specs.append(pl.BlockSpec((None, t, LANES), lambda b, h, i: (b, i, h)))
        else:
            specs.append(pl.BlockSpec((None, None, t, LANES), lambda b, h, i: (b, h, i, 0)))
    ins.append(k_arr)
    specs.append(pl.BlockSpec((None, S, LANES), lambda b, h, i: (b, 0, kb + h)))
    if kaug is not None:
        ins.append(kaug)
        if kaug.ndim == 3:
            specs.append(pl.BlockSpec((None, S, LANES), lambda b, h, i: (b, 0, h)))
        else:
            specs.append(pl.BlockSpec((S, LANES), lambda b, h, i: (0, 0)))
    ins.append(v_arr)
    specs.append(pl.BlockSpec((None, S, LANES), lambda b, h, i: (b, 0, vb + h)))
    ins.append(bias)
    specs.append(pl.BlockSpec(bias.shape, lambda b, h, i: (0, 0, 0)))
    if gate_mode == "elem":
        gb = gate_col0 // LANES
        ins.append(gate)
        specs.append(pl.BlockSpec((None, t, LANES), lambda b, h, i: (b, i, gb + h)))
    elif gate_mode == "col":
        ins.append(gate)
        specs.append(pl.BlockSpec((None, t, LANES), lambda b, h, i: (b, i, 0)))
    if addin is not None:
        ins.append(addin)
        specs.append(pl.BlockSpec((None, t, qw), lambda b, h, i: (b, i, h)))
    kern = functools.partial(
        _flash_kernel, hs=hs, t=t, has_qaug=qaug is not None, has_kaug=kaug is not None,
        n_off=bias.shape[0], full_causal=full_causal, gate_mode=gate_mode, gate_base=gate_base,
        has_addin=addin is not None)
    return pl.pallas_call(
        kern,
        grid=(B, n_kv, S // t),
        in_specs=specs,
        out_specs=pl.BlockSpec((None, t, qw), lambda b, h, i: (b, i, h)),
        out_shape=jax.ShapeDtypeStruct((B, S, n_kv * qw), out_dtype),
        scratch_shapes=[pltpu.VMEM((hs * t, 1), F32), pltpu.VMEM((hs * t, 1), F32),
                        pltpu.VMEM((hs * t, LANES), F32)],
        compiler_params=_params("parallel", "parallel", "arbitrary"),
        name="flash_attention",
    )(*ins)


def _band_bias(t, weight_of_distance, max_dist):
    n_off = (max_dist + t - 1) // t + 1
    r = np.arange(t)[:, None]
    c = np.arange(t)[None, :]
    tabs = []
    for off in range(n_off):
        w = weight_of_distance(off * t + r - c)
        tabs.append(np.where(w > 0, np.log(np.maximum(w, 1.0)), NEG))
    return jnp.asarray(np.stack(tabs), F32)


def _dilated_weight(d):
    w = np.zeros(d.shape, np.float64)
    for window, dil in DILATED_PATTERNS:
        w += (d >= 0) & (d <= window) & (d % dil == 0)
    return w


def _compress_kernel(x_ref, pe_ref, w1_ref, w2_ref, g_ref, tab_ref, o_ref, *, nc):
    half = CMP_STRIDE * HEAD_DIM
    x = x_ref[...].astype(F32)
    top = (x + pe_ref[:, :half]).astype(BF16)
    bot = (x + pe_ref[:, half:]).astype(BF16)
    a = jnp.dot(top, w1_ref[:half, :], preferred_element_type=F32)
    b = jnp.dot(bot, w1_ref[half:, :], preferred_element_type=F32)
    pre = a + pltpu.roll(b, nc - 1, 0)
    hid = pre * (0.5 * (1.0 + jnp.tanh(np.sqrt(2.0 / np.pi) * (pre + 0.044715 * (pre * pre * pre)))))
    y = jnp.dot(hid.astype(BF16), w2_ref[...], preferred_element_type=F32)

    @pl.when(pl.program_id(0) == 0)
    def _():
        ms = jnp.mean(y * y, axis=-1, keepdims=True)
        o_ref[...] = _rope(y * lax.rsqrt(ms + EPS) * g_ref[...], tab_ref).astype(BF16)

    @pl.when(pl.program_id(0) != 0)
    def _():
        o_ref[...] = y.astype(BF16)


def _compress(x2, pe, w1, w2, g_kc, tabs_c):
    _, BG, nc, width = x2.shape
    return pl.pallas_call(
        functools.partial(_compress_kernel, nc=nc),
        grid=(2, BG),
        in_specs=[
            pl.BlockSpec((None, None, nc, width), lambda s, i: (s, i, 0, 0)),
            pl.BlockSpec((None, 1, 2 * width), lambda s, i: (s, 0, 0)),
            pl.BlockSpec((None, 2 * width, HEAD_DIM), lambda s, i: (s, 0, 0)),
            pl.BlockSpec((None, HEAD_DIM, HEAD_DIM), lambda s, i: (s, 0, 0)),
            pl.BlockSpec((1, HEAD_DIM), lambda s, i: (0, 0)),
            pl.BlockSpec((3, nc, LANES), lambda s, i: (0, 0, 0)),
        ],
        out_specs=pl.BlockSpec((None, None, nc, HEAD_DIM), lambda s, i: (s, i, 0, 0)),
        out_shape=jax.ShapeDtypeStruct((2, BG, nc, HEAD_DIM), BF16),
        compiler_params=_params("arbitrary", "arbitrary"),
        name="nsa_compress",
    )(x2, pe, w1, w2, g_kc.reshape(1, HEAD_DIM), tabs_c)


def _cmp_topk_kernel(q_ref, kc_ref, vc_ref, gate_ref, ov_ref, eye_ref, oc_ref, sel_ref, v_scr, *,
                     hs, t, nc, n_slc, n_sel):
    g = pl.program_id(1)
    q0 = pl.program_id(2) * t
    M = hs * t
    q = jnp.concatenate([q_ref[:, h * LANES:(h + 1) * LANES] for h in range(hs)], axis=0)
    s = lax.dot_general(q, kc_ref[...], _NT, preferred_element_type=F32)
    qpos = q0 + (lax.broadcasted_iota(jnp.int32, (M, nc), 0) & (t - 1))
    n = lax.broadcasted_iota(jnp.int32, (M, nc), 1)
    mask = n * CMP_STRIDE + (CMP_BLOCK - 1) <= qpos
    s = jnp.where(mask, s, NEG)
    m = jnp.max(s, axis=1, keepdims=True)
    e = jnp.where(mask, jnp.exp(s - m), 0.0)
    p = e / jnp.maximum(jnp.sum(e, axis=1, keepdims=True), 1e-30)
    oc = jnp.dot(p.astype(BF16), vc_ref[...], preferred_element_type=F32)
    lane = lax.broadcasted_iota(jnp.int32, (t, LANES), 1)
    gates = gate_ref[...]
    psum = jnp.zeros((t, nc), F32)
    for h in range(hs):
        idx = (g * hs + h) * 3
        gcol = jnp.sum(jnp.where(lane == idx, gates, 0.0), axis=1, keepdims=True)
        oc_ref[:, h * LANES:(h + 1) * LANES] = oc[h * t:(h + 1) * t, :] * gcol
        psum = psum + p[h * t:(h + 1) * t, :]

    imp = jnp.zeros((LANES, t), F32)
    for part in _split3(psum):
        imp = imp + lax.dot_general(ov_ref[...], part, _NT, preferred_element_type=F32)
    jrow = lax.broadcasted_iota(jnp.int32, (LANES, t), 0)
    cur = (q0 + lax.broadcasted_iota(jnp.int32, (LANES, t), 1)) // SLC_BLOCK
    forced = (jrow == 0) | (jrow == cur) | (jrow == cur - 1)
    val = jnp.where(forced, FORCED_SCORE, imp)
    val = jnp.where(jrow <= cur, val, NEG)
    v_scr[...] = val

    def body(i, rank):
        vi = v_scr[pl.ds(i, 1), :]
        ge = jnp.where(vi >= val, 1.0, 0.0)
        gt = jnp.where(vi > val, 1.0, 0.0)
        return rank + jnp.where(jrow > i, ge, gt)

    rank = lax.fori_loop(0, n_slc, body, jnp.zeros((LANES, t), F32))
    sel = (rank < n_sel) & (jrow <= cur)
    selb = jnp.where(sel | (jrow >= n_slc), 0.0, NEG).astype(BF16)
    sel_ref[...] = lax.dot_general(eye_ref[...], selb, _NT, preferred_element_type=F32).astype(BF16)


def _cmp_topk(q_arr, kvc, gates, *, batch, seq, hs, t):
    G = N_KV_NSA
    nc = kvc.shape[2]
    n_slc = seq // SLC_BLOCK
    assert n_slc <= LANES and nc == seq // CMP_STRIDE
    qw = hs * LANES
    start = np.arange(nc) * CMP_STRIDE
    js = np.arange(LANES) * SLC_BLOCK
    ov = ((start[None, :] < js[:, None] + SLC_BLOCK) & (start[None, :] + CMP_BLOCK > js[:, None]))
    ov = ov & (np.arange(nc)[None, :] < nc - 1) & (np.arange(LANES)[:, None] < n_slc)
    kern = functools.partial(_cmp_topk_kernel, hs=hs, t=t, nc=nc, n_slc=n_slc,
                             n_sel=min(N_SELECT, n_slc))
    return pl.pallas_call(
        kern,
        grid=(batch, G, seq // t),
        in_specs=[
            pl.BlockSpec((None, t, qw), lambda b, g, i: (b, i, g)),
            pl.BlockSpec((None, None, nc, HEAD_DIM), lambda b, g, i: (0, b * G + g, 0, 0)),
            pl.BlockSpec((None, None, nc, HEAD_DIM), lambda b, g, i: (1, b * G + g, 0, 0)),
            pl.BlockSpec((None, t, LANES), lambda b, g, i: (b, i, 0)),
            pl.BlockSpec((LANES, nc), lambda b, g, i: (0, 0)),
            pl.BlockSpec((t, t), lambda b, g, i: (0, 0)),
        ],
        out_specs=[
            pl.BlockSpec((None, t, qw), lambda b, g, i: (b, i, g)),
            pl.BlockSpec((None, None, t, LANES), lambda b, g, i: (b, g, i, 0)),
        ],
        out_shape=[jax.ShapeDtypeStruct((batch, seq, G * qw), F32),
                   jax.ShapeDtypeStruct((batch, G, seq, LANES), BF16)],
        scratch_shapes=[pltpu.VMEM((LANES, t), F32)],
        compiler_params=_params("parallel", "parallel", "arbitrary"),
        name="nsa_cmp_topk",
    )(q_arr, kvc, kvc, gates, jnp.asarray(ov, BF16), jnp.eye(t, dtype=BF16))


def _rope_tabs(pos):
    half = ROPE_DIM // 2
    inv = 1.0 / (ROPE_THETA ** (jnp.arange(0, ROPE_DIM, 2, dtype=F32) / ROPE_DIM))
    ang = pos.astype(F32)[:, None] * inv
    cos, sin = jnp.cos(ang), jnp.sin(ang)
    n = pos.shape[0]
    c = jnp.ones((n, LANES), F32).at[:, :half].set(cos).at[:, half:ROPE_DIM].set(cos)
    sa = jnp.zeros((n, LANES), F32).at[:, :half].set(-sin)
    sb = jnp.zeros((n, LANES), F32).at[:, half:ROPE_DIM].set(sin)
    return jnp.stack([c, sa, sb])


def _pad_rows(g, rows=8):
    return jnp.zeros((rows, g.shape[-1]), F32).at[:g.shape[0]].set(g)


def _pad_cols(w, n):
    return jnp.zeros((w.shape[0], n), w.dtype).at[:, :w.shape[1]].set(w)


def kernel(x, ln_mix_g, ln_mlp_g, w_mlp_up, w_mlp_down, even_w_in, even_b_f, even_w_out,
           even_g_q_fox, even_g_k_fox, even_g_q_dil, even_g_k_dil, odd_w_in, odd_w_out,
           odd_phi_k_pe, odd_phi_k_w1, odd_phi_k_w2, odd_phi_v_pe, odd_phi_v_w1, odd_phi_v_w2,
           odd_g_q, odd_g_kc, odd_g_ks, odd_g_kw):
    B, S, D = x.shape
    T = B * S
    n_heads = D // HEAD_DIM
    hf = n_heads // 2
    hd = n_heads - hf
    fw, dw = hf * HEAD_DIM, hd * HEAD_DIM
    G = N_KV_NSA
    hpg = n_heads // G
    qw, kvw = n_heads * HEAD_DIM, G * HEAD_DIM
    depth = ln_mix_g.shape[0]

    tm = 512
    t_fox = 512
    t_band = 256
    t_cmp = 128
    tabs = _rope_tabs(jnp.arange(S))
    nc = S // CMP_STRIDE
    tabs_c = _rope_tabs(jnp.arange(nc) * CMP_STRIDE + CMP_BLOCK - 1)

    causal_bias = lambda t: _band_bias(t, lambda d: (d >= 0).astype(np.float64), 0)
    dil_bias = _band_bias(t_band, _dilated_weight, max(w for w, _ in DILATED_PATTERNS))
    win_bias = _band_bias(t_band, lambda d: ((d >= 0) & (d < WINDOW_NSA)).astype(np.float64),
                          WINDOW_NSA - 1)
    onehot_blk = jnp.asarray(
        (np.arange(S)[:, None] // SLC_BLOCK == np.arange(LANES)[None, :]).astype(np.float32), BF16)

    h = x.reshape(T, D)
    for layer in range(depth):
        i = layer // 2
        if layer % 2 == 0:
            w_in = even_w_in[i]
            w_main = jnp.concatenate([w_in[:, :4 * fw], w_in[:, 4 * fw + hf:]], axis=1).astype(BF16)
            w_f = _pad_cols(w_in[:, 4 * fw:4 * fw + hf], LANES).astype(BF16)
            gains = _pad_rows(jnp.stack([even_g_q_fox[i], even_g_k_fox[i], even_g_q_dil[i],
                                         even_g_k_dil[i]]))
            segs = [Seg(fw, norm=True, gain=0, scale=SCALE), Seg(fw, norm=True, gain=1), Seg(fw),
                    Seg(fw, sigmoid=True),
                    Seg(dw, norm=True, gain=2, rope=True, scale=SCALE),
                    Seg(dw, norm=True, gain=3, rope=True), Seg(dw)]
            tn = min(512, fw, dw)
            proj = _norm_proj(h, ln_mix_g[layer], w_main, gains, tabs, segs, seq=S, tm=tm, tn=tn,
                              out_dtype=BF16).reshape(B, S, -1)
            fproj = _norm_proj(h, ln_mix_g[layer], w_f, gains, tabs, [Seg(LANES)], seq=S, tm=tm,
                               tn=LANES, out_dtype=F32)
            qaug, kaug = _fox_aug(fproj, even_b_f[i], batch=B, seq=S, hf=hf, ts=512)
            o_a = _flash(proj, 0, proj, fw, proj, 2 * fw, causal_bias(t_fox), n_kv=hf, hs=1,
                         t=t_fox, out_dtype=BF16, qaug=qaug.reshape(B, S, -1),
                         kaug=kaug.reshape(B, S, -1), full_causal=True, gate=proj,
                         gate_mode="elem", gate_col0=3 * fw)
            o_b = _flash(proj, 4 * fw, proj, 4 * fw + dw, proj, 4 * fw + 2 * dw, dil_bias, n_kv=hd,
                         hs=1, t=t_band, out_dtype=BF16)
            w_out = even_w_out[i].astype(BF16)
            h = _out_proj([o_a.reshape(T, fw), o_b.reshape(T, dw)], [w_out[:fw], w_out[fw:]], h,
                          tm=tm, tn=512)
        else:
            w_in = odd_w_in[i]
            n_main = qw + 6 * kvw
            w_main = w_in[:, :n_main].astype(BF16)
            w_g = _pad_cols(w_in[:, n_main:], LANES).astype(BF16)
            gains = _pad_rows(jnp.stack([odd_g_q[i], odd_g_ks[i], odd_g_kw[i]]))
            segs = [Seg(qw, norm=True, gain=0, rope=True, scale=SCALE), Seg(kvw), Seg(kvw),
                    Seg(kvw, norm=True, gain=1, rope=True), Seg(kvw),
                    Seg(kvw, norm=True, gain=2, rope=True), Seg(kvw)]
            proj = _norm_proj(h, ln_mix_g[layer], w_main, gains, tabs, segs, seq=S, tm=tm,
                              tn=min(512, kvw), out_dtype=BF16).reshape(B, S, -1)
            gates = _norm_proj(h, ln_mix_g[layer], w_g, gains, tabs, [Seg(LANES, sigmoid=True)],
                               seq=S, tm=tm, tn=LANES, out_dtype=F32).reshape(B, S, LANES)

            def chunks(col0):
                xt = proj[:, :, col0:col0 + kvw].reshape(B, S, G, HEAD_DIM).transpose(0, 2, 1, 3)
                return xt.reshape(B * G, nc, CMP_STRIDE * HEAD_DIM)

            x2 = jnp.stack([chunks(qw), chunks(qw + kvw)])
            pe = jnp.stack([odd_phi_k_pe[i], odd_phi_v_pe[i]]).reshape(2, 1, CMP_BLOCK * HEAD_DIM)
            w1 = jnp.stack([odd_phi_k_w1[i], odd_phi_v_w1[i]]).astype(BF16)
            w2 = jnp.stack([odd_phi_k_w2[i], odd_phi_v_w2[i]]).astype(BF16)
            kvc = _compress(x2, pe, w1, w2, odd_g_kc[i], tabs_c)
            o_c, selb = _cmp_topk(proj, kvc, gates, batch=B, seq=S, hs=hpg, t=t_cmp)
            o_cs = _flash(proj, 0, proj, qw + 2 * kvw, proj, qw + 3 * kvw, causal_bias(t_band),
                          n_kv=G, hs=hpg, t=t_band, out_dtype=F32, qaug=selb, kaug=onehot_blk,
                          full_causal=True, gate=gates, gate_mode="col", gate_base=1, addin=o_c)
            o = _flash(proj, 0, proj, qw + 4 * kvw, proj, qw + 5 * kvw, win_bias, n_kv=G, hs=hpg,
                       t=t_band, out_dtype=BF16, gate=gates, gate_mode="col", gate_base=2,
                       addin=o_cs)
            h = _out_proj([o.reshape(T, qw)], [odd_w_out[i].astype(BF16)], h, tm=tm, tn=512)
        h = _mlp(h, ln_mlp_g[layer], w_mlp_up[layer].astype(BF16), w_mlp_down[layer].astype(BF16),
                 tm=tm, tf=512)
    return h.reshape(B, S, D)
```

```python
import functools
from typing import NamedTuple

import numpy as np
import jax
import jax.numpy as jnp
from jax import lax
from jax.experimental import pallas as pl
from jax.experimental.pallas import tpu as pltpu

HEAD_DIM = 128
LANES = 128
N_KV_NSA = 4
DILATED_PATTERNS = ((128, 1), (512, 4), (2048, 16))
CMP_BLOCK = 32
CMP_STRIDE = 16
SLC_BLOCK = 64
N_SELECT = 16
WINDOW_NSA = 512
ROPE_THETA = 500000.0
ROPE_DIM = HEAD_DIM // 4
EPS = 1e-6
NEG = -1e30
FORCED_SCORE = 1e9
SCALE = HEAD_DIM ** -0.5
LOG2E = float(np.log2(np.e))
QSCALE = SCALE * LOG2E
VMEM_LIMIT = 56 * 1024 * 1024

BF16 = jnp.bfloat16
F32 = jnp.float32
_NT = (((1,), (1,)), ((), ()))


def _params(*sem):
    return pltpu.CompilerParams(dimension_semantics=sem, vmem_limit_bytes=VMEM_LIMIT)


def _split3(x):
    hi = x.astype(BF16)
    r1 = x - hi.astype(F32)
    mid = r1.astype(BF16)
    lo = (r1 - mid.astype(F32)).astype(BF16)
    return hi, mid, lo


class Seg(NamedTuple):
    width: int
    norm: bool = False
    gain: int = 0
    rope: bool = False
    scale: float = 1.0
    sigmoid: bool = False


def _rope(y, tab_ref):
    return (y * tab_ref[0] + pltpu.roll(y, LANES - ROPE_DIM // 2, 1) * tab_ref[1]
            + pltpu.roll(y, ROPE_DIM // 2, 1) * tab_ref[2])


def _head_epilogue(y, seg, gains_ref, tab_ref):
    if seg.norm:
        ms = jnp.mean(y * y, axis=-1, keepdims=True)
        y = y * lax.rsqrt(ms + EPS) * gains_ref[seg.gain:seg.gain + 1, :]
    if seg.rope:
        y = _rope(y, tab_ref)
    if seg.scale != 1.0:
        y = y * seg.scale
    if seg.sigmoid:
        y = 1.0 / (1.0 + jnp.exp(-y))
    return y


def _proj_kernel(x_ref, g_ref, w_ref, gains_ref, tab_ref, o_ref, xn_ref, *, segs, tn):
    j = pl.program_id(1)

    @pl.when(j == 0)
    def _():
        x = x_ref[...]
        ms = jnp.mean(x * x, axis=-1, keepdims=True)
        xn_ref[...] = (x * lax.rsqrt(ms + EPS) * g_ref[...]).astype(BF16)

    acc = jnp.dot(xn_ref[...], w_ref[...], preferred_element_type=F32)
    lo = 0
    for seg in segs:
        nt = seg.width // tn

        @pl.when((j >= lo) & (j < lo + nt))
        def _(seg=seg):
            for hd in range(tn // LANES):
                sl = slice(hd * LANES, (hd + 1) * LANES)
                o_ref[:, sl] = _head_epilogue(acc[:, sl], seg, gains_ref, tab_ref).astype(o_ref.dtype)

        lo += nt


def _norm_proj(x, g, w, gains, tabs, segs, *, seq, tm, tn, out_dtype):
    T, D = x.shape
    N = w.shape[1]
    assert sum(s.width for s in segs) == N and all(s.width % tn == 0 for s in segs)
    assert T % tm == 0 and seq % tm == 0 and tn % LANES == 0
    nseq = seq // tm
    return pl.pallas_call(
        functools.partial(_proj_kernel, segs=tuple(segs), tn=tn),
        grid=(T // tm, N // tn),
        in_specs=[
            pl.BlockSpec((tm, D), lambda i, j: (i, 0)),
            pl.BlockSpec((1, D), lambda i, j: (0, 0)),
            pl.BlockSpec((D, tn), lambda i, j: (0, j)),
            pl.BlockSpec(gains.shape, lambda i, j: (0, 0)),
            pl.BlockSpec((3, tm, LANES), lambda i, j: (0, i % nseq, 0)),
        ],
        out_specs=pl.BlockSpec((tm, tn), lambda i, j: (i, j)),
        out_shape=jax.ShapeDtypeStruct((T, N), out_dtype),
        scratch_shapes=[pltpu.VMEM((tm, D), BF16)],
        compiler_params=_params("parallel", "arbitrary"),
        name="norm_proj",
    )(x, g.reshape(1, D), w, gains, tabs)


def _outproj_kernel(*refs, n_parts):
    a_refs = refs[:n_parts]
    w_refs = refs[n_parts:2 * n_parts]
    h_ref = refs[2 * n_parts]
    o_ref = refs[2 * n_parts + 1]
    acc = h_ref[...]
    for a_ref, w_ref in zip(a_refs, w_refs):
        acc = acc + jnp.dot(a_ref[...], w_ref[...], preferred_element_type=F32)
    o_ref[...] = acc


def _out_proj(parts, weights, h, *, tm, tn):
    T, D = h.shape
    n = len(parts)
    in_specs = [pl.BlockSpec((tm, p.shape[1]), lambda i, j: (i, 0)) for p in parts]
    in_specs += [pl.BlockSpec((w.shape[0], tn), lambda i, j: (0, j)) for w in weights]
    in_specs += [pl.BlockSpec((tm, tn), lambda i, j: (i, j))]
    return pl.pallas_call(
        functools.partial(_outproj_kernel, n_parts=n),
        grid=(T // tm, D // tn),
        in_specs=in_specs,
        out_specs=pl.BlockSpec((tm, tn), lambda i, j: (i, j)),
        out_shape=jax.ShapeDtypeStruct((T, D), F32),
        compiler_params=_params("parallel", "arbitrary"),
        name="out_proj",
    )(*parts, *weights, h)


def _mlp_kernel(h_ref, g_ref, wu_ref, wd_ref, o_ref, xn_ref):
    f = pl.program_id(1)

    @pl.when(f == 0)
    def _():
        x = h_ref[...]
        ms = jnp.mean(x * x, axis=-1, keepdims=True)
        xn_ref[...] = (x * lax.rsqrt(ms + EPS) * g_ref[...]).astype(BF16)
        o_ref[...] = x

    u = jnp.dot(xn_ref[...], wu_ref[...], preferred_element_type=F32)
    a = jnp.square(jnp.maximum(u, 0.0)).astype(BF16)
    o_ref[...] += jnp.dot(a, wd_ref[...], preferred_element_type=F32)


def _mlp(h, g, w_up, w_down, *, tm, tf):
    T, D = h.shape
    FF = w_up.shape[1]
    return pl.pallas_call(
        _mlp_kernel,
        grid=(T // tm, FF // tf),
        in_specs=[
            pl.BlockSpec((tm, D), lambda i, f: (i, 0)),
            pl.BlockSpec((1, D), lambda i, f: (0, 0)),
            pl.BlockSpec((D, tf), lambda i, f: (0, f)),
            pl.BlockSpec((tf, D), lambda i, f: (f, 0)),
        ],
        out_specs=pl.BlockSpec((tm, D), lambda i, f: (i, 0)),
        out_shape=jax.ShapeDtypeStruct((T, D), F32),
        scratch_shapes=[pltpu.VMEM((tm, D), BF16)],
        compiler_params=_params("parallel", "arbitrary"),
        name="sq_relu_mlp",
    )(h, g.reshape(1, D), w_up, w_down)


def _foxaug_kernel(f_ref, bf_ref, tri_ref, pq_ref, pk_ref, oq_ref, ok_ref, carry_ref, *, ts):
    @pl.when(pl.program_id(1) == 0)
    def _():
        carry_ref[...] = jnp.zeros_like(carry_ref)

    x = f_ref[...] + bf_ref[...]
    logf = jnp.minimum(x, 0.0) - jnp.log(1.0 + jnp.exp(-jnp.abs(x)))
    tri = tri_ref[...]
    c = carry_ref[...]
    for part in _split3(logf):
        c = c + jnp.dot(tri, part, preferred_element_type=F32)
    carry_ref[...] = c[ts - 1:ts, :]
    parts = jnp.concatenate(list(_split3(c * LOG2E)) + [jnp.ones((ts, LANES), BF16)], axis=1)
    oq_ref[...] = jnp.dot(parts, pq_ref[...], preferred_element_type=F32).astype(BF16)
    ok_ref[...] = jnp.dot(parts, pk_ref[...], preferred_element_type=F32).astype(BF16)


def _fox_aug(fproj, b_f, *, batch, seq, hf, ts):
    T = fproj.shape[0]
    ns = seq // ts
    tri = np.tril(np.ones((ts, ts), np.float32))
    pq = np.zeros((4 * LANES, hf * LANES), np.float32)
    pk = np.zeros((4 * LANES, hf * LANES), np.float32)
    for h in range(hf):
        for p in range(3):
            pq[p * LANES + h, h * LANES + p] = 1.0
            pq[3 * LANES, h * LANES + 3 + p] = 1.0
            pk[3 * LANES, h * LANES + p] = 1.0
            pk[p * LANES + h, h * LANES + 3 + p] = -1.0
    bf = jnp.zeros((1, LANES), F32).at[0, :hf].set(b_f)
    out = jax.ShapeDtypeStruct((T, hf * LANES), BF16)
    return pl.pallas_call(
        functools.partial(_foxaug_kernel, ts=ts),
        grid=(batch, ns),
        in_specs=[
            pl.BlockSpec((ts, LANES), lambda b, s: (b * ns + s, 0)),
            pl.BlockSpec((1, LANES), lambda b, s: (0, 0)),
            pl.BlockSpec((ts, ts), lambda b, s: (0, 0)),
            pl.BlockSpec(pq.shape, lambda b, s: (0, 0)),
            pl.BlockSpec(pk.shape, lambda b, s: (0, 0)),
        ],
        out_specs=[pl.BlockSpec((ts, hf * LANES), lambda b, s: (b * ns + s, 0))] * 2,
        out_shape=[out, out],
        scratch_shapes=[pltpu.VMEM((1, LANES), F32)],
        compiler_params=_params("parallel", "arbitrary"),
        name="fox_aug",
    )(fproj, bf, jnp.asarray(tri, BF16), jnp.asarray(pq, BF16), jnp.asarray(pk, BF16))


def _flash_kernel(*refs, hs, tq, tk, mode, has_aug, gate_mode, gate_base, has_addin):
    it = iter(refs)
    q_ref = next(it)
    qa_ref = next(it) if has_aug else None
    k_ref = next(it)
    ka_ref = next(it) if has_aug else None
    v_ref = next(it)
    bias_ref = next(it)
    gate_ref = next(it) if gate_mode else None
    add_ref = next(it) if has_addin else None
    o_ref = next(it)
    q_scr = next(it)

    qi = pl.program_id(2)
    M = hs * tq
    for h in range(hs):
        q_scr[h * tq:(h + 1) * tq, :LANES] = q_ref[:, h * LANES:(h + 1) * LANES]
        if has_aug:
            q_scr[h * tq:(h + 1) * tq, LANES:] = qa_ref[...]
    ones = jnp.ones((tk, LANES), BF16)

    def scores(k0):
        k = k_ref[pl.ds(k0, tk), :]
        if has_aug:
            k = jnp.concatenate([k, ka_ref[pl.ds(k0, tk), :]], axis=1)
        return lax.dot_general(q_scr[...], k, _NT, preferred_element_type=F32)

    def biased(s, off):
        b = bias_ref[off]
        return (s.reshape(hs, tq, tk) + b[None]).reshape(M, tk) if hs > 1 else s + b

    def values(k0):
        return jnp.concatenate([v_ref[pl.ds(k0, tk), :], ones], axis=1)

    if mode == "window":
        back = tk - tq
        k0 = pl.multiple_of(jnp.maximum(qi * tq - back, 0), tq)
        s = biased(scores(k0), jnp.minimum(qi, back // tq))
        p = jnp.exp2(s - jnp.max(s, axis=1, keepdims=True))
        acc = jnp.dot(p.astype(BF16), values(k0), preferred_element_type=F32)
    else:
        m_scr, acc_scr, s_scr = next(it), next(it), next(it)
        m_scr[...] = jnp.full(m_scr.shape, NEG, F32)
        acc_scr[...] = jnp.zeros(acc_scr.shape, F32)

        def stage(kt):
            s_scr[kt & 1] = scores(pl.multiple_of(kt * tk, tk))

        def update(kt, off):
            s = s_scr[kt & 1]
            if off is not None:
                s = biased(s, off)
            m_prev = m_scr[...]
            m_new = jnp.maximum(m_prev, jnp.max(s, axis=1, keepdims=True))
            alpha = jnp.exp2(m_prev - m_new)
            p = jnp.exp2(s - jnp.concatenate([m_new] * (tk // LANES), axis=1))
            pv = jnp.dot(p.astype(BF16), values(pl.multiple_of(kt * tk, tk)),
                         preferred_element_type=F32)
            acc_scr[...] = jnp.concatenate([alpha, alpha], axis=1) * acc_scr[...] + pv
            m_scr[...] = m_new

        if mode == "causal":
            last = (qi * tq) // tk
            first, last_off = 0, qi - last * (tk // tq)
        else:
            last = qi
            first, last_off = jnp.maximum(qi - (bias_ref.shape[0] - 1), 0), 0

        stage(first)

        def body(kt, c):
            update(kt, None if mode == "causal" else qi - kt)
            stage(kt + 1)
            return c

        lax.fori_loop(first, last, body, 0)
        update(last, last_off)
        acc = acc_scr[...]

    o = acc[:, :LANES] * (1.0 / acc[:, LANES:])
    for h in range(hs):
        oh = o[h * tq:(h + 1) * tq, :]
        if gate_mode == "elem":
            oh = oh * gate_ref[...].astype(F32)
        elif gate_mode == "col":
            idx = (pl.program_id(1) * hs + h) * 3 + gate_base
            lane = lax.broadcasted_iota(jnp.int32, (tq, LANES), 1)
            oh = oh * jnp.sum(jnp.where(lane == idx, gate_ref[...], 0.0), axis=1, keepdims=True)
        sl = slice(h * LANES, (h + 1) * LANES)
        if has_addin:
            oh = oh + add_ref[:, sl]
        o_ref[:, sl] = oh.astype(o_ref.dtype)


def _flash(q_arr, q_col0, k_arr, k_col0, v_arr, v_col0, bias, *, n_kv, hs, tq, tk, mode, out_dtype,
           qaug=None, kaug=None, gate=None, gate_mode=None, gate_col0=0, gate_base=0, addin=None):
    B, S, _ = q_arr.shape
    qw = hs * LANES
    assert q_col0 % qw == 0 and k_col0 % LANES == 0 and v_col0 % LANES == 0
    assert S % tq == 0 and S % tk == 0 if mode != "window" else (S % tq == 0 and tk <= S)
    assert tk % tq == 0 and (mode != "band" or tq == tk) and bias.shape[1:] == (tq, tk)
    qb, kb, vb = q_col0 // qw, k_col0 // LANES, v_col0 // LANES
    assert (qaug is None) == (kaug is None)
    dk = LANES if qaug is None else 2 * LANES
    ins, specs = [q_arr], [pl.BlockSpec((None, tq, qw), lambda b, h, i: (b, i, qb + h))]
    if qaug is not None:
        ins.append(qaug)
        if qaug.ndim == 3:
            specs.append(pl.BlockSpec((None, tq, LANES), lambda b, h, i: (b, i, h)))
        else:
            specs.append(pl.BlockSpec((None, None, tq, LANES), lambda b, h, i: (b, h, i, 0)))
    ins.append(k_arr)
    specs.append(pl.BlockSpec((None, S, LANES), lambda b, h, i: (b, 0, kb + h)))
    if kaug is not None:
        ins.append(kaug)
        if kaug.ndim == 3:
            specs.append(pl.BlockSpec((None, S, LANES), lambda b, h, i: (b, 0, h)))
        else:
            specs.append(pl.BlockSpec((S, LANES), lambda b, h, i: (0, 0)))
    ins.append(v_arr)
    specs.append(pl.BlockSpec((None, S, LANES), lambda b, h, i: (b, 0, vb + h)))
    ins.append(bias)
    specs.append(pl.BlockSpec(bias.shape, lambda b, h, i: (0, 0, 0)))
    if gate_mode == "elem":
        gb = gate_col0 // LANES
        ins.append(gate)
        specs.append(pl.BlockSpec((None, tq, LANES), lambda b, h, i: (b, i, gb + h)))
    elif gate_mode == "col":
        ins.append(gate)
        specs.append(pl.BlockSpec((None, tq, LANES), lambda b, h, i: (b, i, 0)))
    if addin is not None:
        ins.append(addin)
        specs.append(pl.BlockSpec((None, tq, qw), lambda b, h, i: (b, i, h)))
    kern = functools.partial(
        _flash_kernel, hs=hs, tq=tq, tk=tk, mode=mode, has_aug=qaug is not None,
        gate_mode=gate_mode, gate_base=gate_base, has_addin=addin is not None)
    scratch = [pltpu.VMEM((hs * tq, dk), BF16)]
    if mode != "window":
        scratch += [pltpu.VMEM((hs * tq, LANES), F32), pltpu.VMEM((hs * tq, 2 * LANES), F32),
                    pltpu.VMEM((2, hs * tq, tk), F32)]
    return pl.pallas_call(
        kern,
        grid=(B, n_kv, S // tq),
        in_specs=specs,
        out_specs=pl.BlockSpec((None, tq, qw), lambda b, h, i: (b, i, h)),
        out_shape=jax.ShapeDtypeStruct((B, S, n_kv * qw), out_dtype),
        scratch_shapes=scratch,
        compiler_params=_params("parallel", "parallel", "arbitrary"),
        name="flash_attention",
    )(*ins)


def _bias_table(tq, tk, n, weight_of_distance):
    r = np.arange(tq)[:, None]
    c = np.arange(tk)[None, :]
    tabs = []
    for v in range(n):
        w = weight_of_distance(v * tq + r - c)
        tabs.append(np.where(w > 0, np.log2(np.maximum(w, 1.0)), NEG))
    return jnp.asarray(np.stack(tabs), F32)


def _dilated_weight(d):
    w = np.zeros(d.shape, np.float64)
    for window, dil in DILATED_PATTERNS:
        w += (d >= 0) & (d <= window) & (d % dil == 0)
    return w


def _compress_kernel(x_ref, pe_ref, w1_ref, w2_ref, g_ref, tab_ref, o_ref, *, nc):
    half = CMP_STRIDE * HEAD_DIM
    x = x_ref[...].astype(F32)
    top = (x + pe_ref[:, :half]).astype(BF16)
    bot = (x + pe_ref[:, half:]).astype(BF16)
    a = jnp.dot(top, w1_ref[:half, :], preferred_element_type=F32)
    b = jnp.dot(bot, w1_ref[half:, :], preferred_element_type=F32)
    pre = a + pltpu.roll(b, nc - 1, 0)
    hid = pre * (0.5 * (1.0 + jnp.tanh(np.sqrt(2.0 / np.pi) * (pre + 0.044715 * (pre * pre * pre)))))
    y = jnp.dot(hid.astype(BF16), w2_ref[...], preferred_element_type=F32)

    @pl.when(pl.program_id(0) == 0)
    def _():
        ms = jnp.mean(y * y, axis=-1, keepdims=True)
        o_ref[...] = _rope(y * lax.rsqrt(ms + EPS) * g_ref[...], tab_ref).astype(BF16)

    @pl.when(pl.program_id(0) != 0)
    def _():
        o_ref[...] = y.astype(BF16)


def _compress(x2, pe, w1, w2, g_kc, tabs_c):
    _, BG, nc, width = x2.shape
    return pl.pallas_call(
        functools.partial(_compress_kernel, nc=nc),
        grid=(2, BG),
        in_specs=[
            pl.BlockSpec((None, None, nc, width), lambda s, i: (s, i, 0, 0)),
            pl.BlockSpec((None, 1, 2 * width), lambda s, i: (s, 0, 0)),
            pl.BlockSpec((None, 2 * width, HEAD_DIM), lambda s, i: (s, 0, 0)),
            pl.BlockSpec((None, HEAD_DIM, HEAD_DIM), lambda s, i: (s, 0, 0)),
            pl.BlockSpec((1, HEAD_DIM), lambda s, i: (0, 0)),
            pl.BlockSpec((3, nc, LANES), lambda s, i: (0, 0, 0)),
        ],
        out_specs=pl.BlockSpec((None, None, nc, HEAD_DIM), lambda s, i: (s, i, 0, 0)),
        out_shape=jax.ShapeDtypeStruct((2, BG, nc, HEAD_DIM), BF16),
        compiler_params=_params("arbitrary", "arbitrary"),
        name="nsa_compress",
    )(x2, pe, w1, w2, g_kc.reshape(1, HEAD_DIM), tabs_c)


def _cmp_topk_kernel(q_ref, kc_ref, vc_ref, gate_ref, ov_ref, eye_ref, oc_ref, sel_ref, v_scr, *,
                     hs, t, nc, n_slc, n_sel):
    g = pl.program_id(1)
    q0 = pl.program_id(2) * t
    M = hs * t
    q = jnp.concatenate([q_ref[:, h * LANES:(h + 1) * LANES] for h in range(hs)], axis=0)
    s = lax.dot_general(q, kc_ref[...], _NT, preferred_element_type=F32)
    qpos = q0 + (lax.broadcasted_iota(jnp.int32, (M, nc), 0) & (t - 1))
    n = lax.broadcasted_iota(jnp.int32, (M, nc), 1)
    mask = n * CMP_STRIDE + (CMP_BLOCK - 1) <= qpos
    s = jnp.where(mask, s, NEG)
    m = jnp.max(s, axis=1, keepdims=True)
    e = jnp.where(mask, jnp.exp2(s - m), 0.0)
    p = e / jnp.maximum(jnp.sum(e, axis=1, keepdims=True), 1e-30)
    oc = jnp.dot(p.astype(BF16), vc_ref[...], preferred_element_type=F32)
    lane = lax.broadcasted_iota(jnp.int32, (t, LANES), 1)
    gates = gate_ref[...]
    psum = jnp.zeros((t, nc), F32)
    for h in range(hs):
        idx = (g * hs + h) * 3
        gcol = jnp.sum(jnp.where(lane == idx, gates, 0.0), axis=1, keepdims=True)
        oc_ref[:, h * LANES:(h + 1) * LANES] = oc[h * t:(h + 1) * t, :] * gcol
        psum = psum + p[h * t:(h + 1) * t, :]

    imp = jnp.zeros((LANES, t), F32)
    for part in _split3(psum):
        imp = imp + lax.dot_general(ov_ref[...], part, _NT, preferred_element_type=F32)
    jrow = lax.broadcasted_iota(jnp.int32, (LANES, t), 0)
    cur = (q0 + lax.broadcasted_iota(jnp.int32, (LANES, t), 1)) // SLC_BLOCK
    forced = (jrow == 0) | (jrow == cur) | (jrow == cur - 1)
    val = jnp.where(forced, FORCED_SCORE, imp)
    val = jnp.where(jrow <= cur, val, NEG)
    v_scr[...] = val

    def body(i, rank):
        vi = v_scr[pl.ds(i, 1), :]
        ge = jnp.where(vi >= val, 1.0, 0.0)
        gt = jnp.where(vi > val, 1.0, 0.0)
        return rank + jnp.where(jrow > i, ge, gt)

    rank = lax.fori_loop(0, n_slc, body, jnp.zeros((LANES, t), F32))
    sel = (rank < n_sel) & (jrow <= cur)
    selb = jnp.where(sel | (jrow >= n_slc), 0.0, NEG).astype(BF16)
    sel_ref[...] = lax.dot_general(eye_ref[...], selb, _NT, preferred_element_type=F32).astype(BF16)


def _cmp_topk(q_arr, kvc, gates, *, batch, seq, hs, t):
    G = N_KV_NSA
    nc = kvc.shape[2]
    n_slc = seq // SLC_BLOCK
    assert n_slc <= LANES and nc == seq // CMP_STRIDE
    qw = hs * LANES
    start = np.arange(nc) * CMP_STRIDE
    js = np.arange(LANES) * SLC_BLOCK
    ov = ((start[None, :] < js[:, None] + SLC_BLOCK) & (start[None, :] + CMP_BLOCK > js[:, None]))
    ov = ov & (np.arange(nc)[None, :] < nc - 1) & (np.arange(LANES)[:, None] < n_slc)
    kern = functools.partial(_cmp_topk_kernel, hs=hs, t=t, nc=nc, n_slc=n_slc,
                             n_sel=min(N_SELECT, n_slc))
    return pl.pallas_call(
        kern,
        grid=(batch, G, seq // t),
        in_specs=[
            pl.BlockSpec((None, t, qw), lambda b, g, i: (b, i, g)),
            pl.BlockSpec((None, None, nc, HEAD_DIM), lambda b, g, i: (0, b * G + g, 0, 0)),
            pl.BlockSpec((None, None, nc, HEAD_DIM), lambda b, g, i: (1, b * G + g, 0, 0)),
            pl.BlockSpec((None, t, LANES), lambda b, g, i: (b, i, 0)),
            pl.BlockSpec((LANES, nc), lambda b, g, i: (0, 0)),
            pl.BlockSpec((t, t), lambda b, g, i: (0, 0)),
        ],
        out_specs=[
            pl.BlockSpec((None, t, qw), lambda b, g, i: (b, i, g)),
            pl.BlockSpec((None, None, t, LANES), lambda b, g, i: (b, g, i, 0)),
        ],
        out_shape=[jax.ShapeDtypeStruct((batch, seq, G * qw), F32),
                   jax.ShapeDtypeStruct((batch, G, seq, LANES), BF16)],
        scratch_shapes=[pltpu.VMEM((LANES, t), F32)],
        compiler_params=_params("parallel", "parallel", "arbitrary"),
        name="nsa_cmp_topk",
    )(q_arr, kvc, kvc, gates, jnp.asarray(ov, BF16), jnp.eye(t, dtype=BF16))


def _rope_tabs(pos):
    half = ROPE_DIM // 2
    inv = 1.0 / (ROPE_THETA ** (jnp.arange(0, ROPE_DIM, 2, dtype=F32) / ROPE_DIM))
    ang = pos.astype(F32)[:, None] * inv
    cos, sin = jnp.cos(ang), jnp.sin(ang)
    n = pos.shape[0]
    c = jnp.ones((n, LANES), F32).at[:, :half].set(cos).at[:, half:ROPE_DIM].set(cos)
    sa = jnp.zeros((n, LANES), F32).at[:, :half].set(-sin)
    sb = jnp.zeros((n, LANES), F32).at[:, half:ROPE_DIM].set(sin)
    return jnp.stack([c, sa, sb])


def _pad_rows(g, rows=8):
    return jnp.zeros((rows, g.shape[-1]), F32).at[:g.shape[0]].set(g)


def _pad_cols(w, n):
    return jnp.zeros((w.shape[0], n), w.dtype).at[:, :w.shape[1]].set(w)


def kernel(x, ln_mix_g, ln_mlp_g, w_mlp_up, w_mlp_down, even_w_in, even_b_f, even_w_out,
           even_g_q_fox, even_g_k_fox, even_g_q_dil, even_g_k_dil, odd_w_in, odd_w_out,
           odd_phi_k_pe, odd_phi_k_w1, odd_phi_k_w2, odd_phi_v_pe, odd_phi_v_w1, odd_phi_v_w2,
           odd_g_q, odd_g_kc, odd_g_ks, odd_g_kw):
    B, S, D = x.shape
    T = B * S
    n_heads = D // HEAD_DIM
    hf = n_heads // 2
    hd = n_heads - hf
    fw, dw = hf * HEAD_DIM, hd * HEAD_DIM
    G = N_KV_NSA
    hpg = n_heads // G
    qw, kvw = n_heads * HEAD_DIM, G * HEAD_DIM
    depth = ln_mix_g.shape[0]

    tm = 512
    t_fox = 512
    t_band = 256
    tk_sel = 512
    t_cmp = 128
    tabs = _rope_tabs(jnp.arange(S))
    nc = S // CMP_STRIDE
    tabs_c = _rope_tabs(jnp.arange(nc) * CMP_STRIDE + CMP_BLOCK - 1)

    causal_w = lambda d: (d >= 0).astype(np.float64)
    win_w = lambda d: ((d >= 0) & (d < WINDOW_NSA)).astype(np.float64)
    dil_span = max(w for w, _ in DILATED_PATTERNS)
    fox_bias = _bias_table(t_fox, t_fox, 1, causal_w)
    dil_bias = _bias_table(t_band, t_band, (dil_span + t_band - 1) // t_band + 1, _dilated_weight)
    sel_bias = _bias_table(t_band, tk_sel, tk_sel // t_band, causal_w)
    win_back = -(-(WINDOW_NSA - 1) // t_band) * t_band
    win_bias = _bias_table(t_band, win_back + t_band, win_back // t_band + 1, win_w)
    onehot_blk = jnp.asarray(
        (np.arange(S)[:, None] // SLC_BLOCK == np.arange(LANES)[None, :]).astype(np.float32), BF16)

    h = x.reshape(T, D)
    for layer in range(depth):
        i = layer // 2
        if layer % 2 == 0:
            w_in = even_w_in[i]
            w_main = jnp.concatenate([w_in[:, :4 * fw], w_in[:, 4 * fw + hf:]], axis=1).astype(BF16)
            w_f = _pad_cols(w_in[:, 4 * fw:4 * fw + hf], LANES).astype(BF16)
            gains = _pad_rows(jnp.stack([even_g_q_fox[i], even_g_k_fox[i], even_g_q_dil[i],
                                         even_g_k_dil[i]]))
            segs = [Seg(fw, norm=True, gain=0, scale=QSCALE), Seg(fw, norm=True, gain=1), Seg(fw),
                    Seg(fw, sigmoid=True),
                    Seg(dw, norm=True, gain=2, rope=True, scale=QSCALE),
                    Seg(dw, norm=True, gain=3, rope=True), Seg(dw)]
            tn = min(512, fw, dw)
            proj = _norm_proj(h, ln_mix_g[layer], w_main, gains, tabs, segs, seq=S, tm=tm, tn=tn,
                              out_dtype=BF16).reshape(B, S, -1)
            fproj = _norm_proj(h, ln_mix_g[layer], w_f, gains, tabs, [Seg(LANES)], seq=S, tm=tm,
                               tn=LANES, out_dtype=F32)
            qaug, kaug = _fox_aug(fproj, even_b_f[i], batch=B, seq=S, hf=hf, ts=512)
            o_a = _flash(proj, 0, proj, fw, proj, 2 * fw, fox_bias, n_kv=hf, hs=1, tq=t_fox,
                         tk=t_fox, mode="causal", out_dtype=BF16, qaug=qaug.reshape(B, S, -1),
                         kaug=kaug.reshape(B, S, -1), gate=proj, gate_mode="elem",
                         gate_col0=3 * fw)
            o_b = _flash(proj, 4 * fw, proj, 4 * fw + dw, proj, 4 * fw + 2 * dw, dil_bias, n_kv=hd,
                         hs=1, tq=t_band, tk=t_band, mode="band", out_dtype=BF16)
            w_out = even_w_out[i].astype(BF16)
            h = _out_proj([o_a.reshape(T, fw), o_b.reshape(T, dw)], [w_out[:fw], w_out[fw:]], h,
                          tm=tm, tn=512)
        else:
            w_in = odd_w_in[i]
            n_main = qw + 6 * kvw
            w_main = w_in[:, :n_main].astype(BF16)
            w_g = _pad_cols(w_in[:, n_main:], LANES).astype(BF16)
            gains = _pad_rows(jnp.stack([odd_g_q[i], odd_g_ks[i], odd_g_kw[i]]))
            segs = [Seg(qw, norm=True, gain=0, rope=True, scale=QSCALE), Seg(kvw), Seg(kvw),
                    Seg(kvw, norm=True, gain=1, rope=True), Seg(kvw),
                    Seg(kvw, norm=True, gain=2, rope=True), Seg(kvw)]
            proj = _norm_proj(h, ln_mix_g[layer], w_main, gains, tabs, segs, seq=S, tm=tm,
                              tn=min(512, kvw), out_dtype=BF16).reshape(B, S, -1)
            gates = _norm_proj(h, ln_mix_g[layer], w_g, gains, tabs, [Seg(LANES, sigmoid=True)],
                               seq=S, tm=tm, tn=LANES, out_dtype=F32).reshape(B, S, LANES)

            def chunks(col0):
                xt = proj[:, :, col0:col0 + kvw].reshape(B, S, G, HEAD_DIM).transpose(0, 2, 1, 3)
                return xt.reshape(B * G, nc, CMP_STRIDE * HEAD_DIM)

            x2 = jnp.stack([chunks(qw), chunks(qw + kvw)])
            pe = jnp.stack([odd_phi_k_pe[i], odd_phi_v_pe[i]]).reshape(2, 1, CMP_BLOCK * HEAD_DIM)
            w1 = jnp.stack([odd_phi_k_w1[i], odd_phi_v_w1[i]]).astype(BF16)
            w2 = jnp.stack([odd_phi_k_w2[i], odd_phi_v_w2[i]]).astype(BF16)
            kvc = _compress(x2, pe, w1, w2, odd_g_kc[i], tabs_c)
            o_c, selb = _cmp_topk(proj, kvc, gates, batch=B, seq=S, hs=hpg, t=t_cmp)
            o_cs = _flash(proj, 0, proj, qw + 2 * kvw, proj, qw + 3 * kvw, sel_bias, n_kv=G,
                          hs=hpg, tq=t_band, tk=tk_sel, mode="causal", out_dtype=F32, qaug=selb,
                          kaug=onehot_blk, gate=gates, gate_mode="col", gate_base=1, addin=o_c)
            o = _flash(proj, 0, proj, qw + 4 * kvw, proj, qw + 5 * kvw, win_bias, n_kv=G, hs=hpg,
                       tq=t_band, tk=win_back + t_band, mode="window", out_dtype=BF16, gate=gates,
                       gate_mode="col", gate_base=2, addin=o_cs)
            h = _out_proj([o.reshape(T, qw)], [odd_w_out[i].astype(BF16)], h, tm=tm, tn=512)
        h = _mlp(h, ln_mlp_g[layer], w_mlp_up[layer].astype(BF16), w_mlp_down[layer].astype(BF16),
                 tm=tm, tf=512)
    return h.reshape(B, S, D)
```

```python
import functools
from typing import NamedTuple

import numpy as np
import jax
import jax.numpy as jnp
from jax import lax
from jax.experimental import pallas as pl
from jax.experimental.pallas import tpu as pltpu

HEAD_DIM = 128
LANES = 128
N_KV_NSA = 4
DILATED_PATTERNS = ((128, 1), (512, 4), (2048, 16))
CMP_BLOCK = 32
CMP_STRIDE = 16
SLC_BLOCK = 64
N_SELECT = 16
WINDOW_NSA = 512
ROPE_THETA = 500000.0
ROPE_DIM = HEAD_DIM // 4
EPS = 1e-6
NEG = -1e30
FORCED_SCORE = 1e9
SCALE = HEAD_DIM ** -0.5
LOG2E = float(np.log2(np.e))
QSCALE = SCALE * LOG2E
VMEM_LIMIT = 56 * 1024 * 1024

BF16 = jnp.bfloat16
F32 = jnp.float32
_NT = (((1,), (1,)), ((), ()))


def _params(*sem):
    return pltpu.CompilerParams(dimension_semantics=sem, vmem_limit_bytes=VMEM_LIMIT)


def _split3(x):
    hi = x.astype(BF16)
    r1 = x - hi.astype(F32)
    mid = r1.astype(BF16)
    lo = (r1 - mid.astype(F32)).astype(BF16)
    return hi, mid, lo


class Seg(NamedTuple):
    width: int
    norm: bool = False
    gain: int = 0
    n_gains: int = 1
    rope: bool = False
    scale: float = 1.0
    sigmoid: bool = False


def _rope(y, tab_ref):
    return y * tab_ref[0] + pltpu.roll(y, LANES // 2, 1) * tab_ref[1]


def _proj_kernel(x_ref, g_ref, w_ref, gains_ref, tab_ref, o_ref, xn_ref, y_scr, r_scr, *, segs, tn):
    j = pl.program_id(1)

    @pl.when(j == 0)
    def _():
        x = x_ref[...]
        ms = jnp.mean(x * x, axis=-1, keepdims=True)
        xn_ref[...] = (x * lax.rsqrt(ms + EPS) * g_ref[...]).astype(BF16)

    sub = min(tn, 2 * LANES)
    lo = 0
    for seg in segs:
        nt = seg.width // tn

        @pl.when((j >= lo) & (j < lo + nt))
        def _(seg=seg, lo=lo):
            heads = [slice(hd * LANES, (hd + 1) * LANES) for hd in range(sub // LANES)]
            for c in range(tn // sub):
                acc = jnp.dot(xn_ref[...], w_ref[:, c * sub:(c + 1) * sub],
                              preferred_element_type=F32)
                if not seg.norm:
                    y = 1.0 / (1.0 + jnp.exp(-acc)) if seg.sigmoid else acc
                    o_ref[:, c * sub:(c + 1) * sub] = y.astype(o_ref.dtype)
                    continue
                ys, rs = y_scr.at[c % 2], r_scr.at[c % 2]
                ys[...] = acc
                for sl in heads:
                    y = ys[:, sl]
                    r = lax.rsqrt(jnp.mean(y * y, axis=-1, keepdims=True) + EPS)
                    rs[:, sl] = jnp.broadcast_to(r, y.shape)
                for hd, sl in enumerate(heads):
                    gi = seg.gain
                    if seg.n_gains > 1:
                        tile = 0 if seg.width == tn else j - lo
                        head = tile * (tn // LANES) + (c * sub) // LANES + hd
                        gi = gi + head // (seg.width // LANES // seg.n_gains)
                    ys[:, sl] = ys[:, sl] * rs[:, sl] * gains_ref[pl.ds(gi, 1), :]
                for hd, sl in enumerate(heads):
                    y = _rope(ys[:, sl], tab_ref) if seg.rope else ys[:, sl]
                    if seg.scale != 1.0:
                        y = y * seg.scale
                    col = c * sub + hd * LANES
                    o_ref[:, col:col + LANES] = y.astype(o_ref.dtype)

        lo += nt


def _norm_proj(x, g, w, gains, tabs, segs, *, seq, tm, tn, out_dtype):
    T, D = x.shape
    N = w.shape[1]
    assert sum(s.width for s in segs) == N and all(s.width % tn == 0 for s in segs)
    assert T % tm == 0 and seq % tm == 0 and tn % LANES == 0
    assert all(s.norm or (not s.rope and s.scale == 1.0) for s in segs)
    assert not any(s.norm and s.sigmoid for s in segs)
    nseq = seq // tm
    sub = min(tn, 2 * LANES)
    return pl.pallas_call(
        functools.partial(_proj_kernel, segs=tuple(segs), tn=tn),
        grid=(T // tm, N // tn),
        in_specs=[
            pl.BlockSpec((tm, D), lambda i, j: (i, 0)),
            pl.BlockSpec((1, D), lambda i, j: (0, 0)),
            pl.BlockSpec((D, tn), lambda i, j: (0, j)),
            pl.BlockSpec(gains.shape, lambda i, j: (0, 0)),
            pl.BlockSpec((2, tm, LANES), lambda i, j: (0, i % nseq, 0)),
        ],
        out_specs=pl.BlockSpec((tm, tn), lambda i, j: (i, j)),
        out_shape=jax.ShapeDtypeStruct((T, N), out_dtype),
        scratch_shapes=[pltpu.VMEM((tm, D), BF16), pltpu.VMEM((2, tm, sub), F32),
                        pltpu.VMEM((2, tm, sub), F32)],
        compiler_params=_params("parallel", "arbitrary"),
        name="norm_proj",
    )(x, g.reshape(1, D), w, gains, tabs)


def _outproj_kernel(*refs, n_parts):
    a_refs = refs[:n_parts]
    w_refs = refs[n_parts:2 * n_parts]
    h_ref = refs[2 * n_parts]
    o_ref = refs[2 * n_parts + 1]
    acc = h_ref[...]
    for a_ref, w_ref in zip(a_refs, w_refs):
        acc = acc + jnp.dot(a_ref[...], w_ref[...], preferred_element_type=F32)
    o_ref[...] = acc


def _out_proj(parts, weights, h, *, tm, tn):
    T, D = h.shape
    n = len(parts)
    in_specs = [pl.BlockSpec((tm, p.shape[1]), lambda i, j: (i, 0)) for p in parts]
    in_specs += [pl.BlockSpec((w.shape[0], tn), lambda i, j: (0, j)) for w in weights]
    in_specs += [pl.BlockSpec((tm, tn), lambda i, j: (i, j))]
    return pl.pallas_call(
        functools.partial(_outproj_kernel, n_parts=n),
        grid=(T // tm, D // tn),
        in_specs=in_specs,
        out_specs=pl.BlockSpec((tm, tn), lambda i, j: (i, j)),
        out_shape=jax.ShapeDtypeStruct((T, D), F32),
        compiler_params=_params("parallel", "arbitrary"),
        name="out_proj",
    )(*parts, *weights, h)


def _mlp_kernel(h_ref, g_ref, wu_ref, wd_ref, o_ref, xn_ref):
    f = pl.program_id(1)

    @pl.when(f == 0)
    def _():
        x = h_ref[...]
        ms = jnp.mean(x * x, axis=-1, keepdims=True)
        xn_ref[...] = (x * lax.rsqrt(ms + EPS) * g_ref[...]).astype(BF16)
        o_ref[...] = x

    tf = wu_ref.shape[1]
    sub = min(tf, 4 * LANES)
    total = None
    for c in range(tf // sub):
        u = jnp.dot(xn_ref[...], wu_ref[:, c * sub:(c + 1) * sub], preferred_element_type=F32)
        a = jnp.square(jnp.maximum(u, 0.0)).astype(BF16)
        part = jnp.dot(a, wd_ref[c * sub:(c + 1) * sub, :], preferred_element_type=F32)
        total = part if total is None else total + part
    o_ref[...] += total


def _mlp(h, g, w_up, w_down, *, tm, tf):
    T, D = h.shape
    FF = w_up.shape[1]
    return pl.pallas_call(
        _mlp_kernel,
        grid=(T // tm, FF // tf),
        in_specs=[
            pl.BlockSpec((tm, D), lambda i, f: (i, 0)),
            pl.BlockSpec((1, D), lambda i, f: (0, 0)),
            pl.BlockSpec((D, tf), lambda i, f: (0, f)),
            pl.BlockSpec((tf, D), lambda i, f: (f, 0)),
        ],
        out_specs=pl.BlockSpec((tm, D), lambda i, f: (i, 0)),
        out_shape=jax.ShapeDtypeStruct((T, D), F32),
        scratch_shapes=[pltpu.VMEM((tm, D), BF16)],
        compiler_params=_params("parallel", "arbitrary"),
        name="sq_relu_mlp",
    )(h, g.reshape(1, D), w_up, w_down)


def _foxaug_kernel(f_ref, bf_ref, tri_ref, pq_ref, pk_ref, oq_ref, ok_ref, carry_ref, *, ts):
    @pl.when(pl.program_id(1) == 0)
    def _():
        carry_ref[...] = jnp.zeros_like(carry_ref)

    x = f_ref[...] + bf_ref[...]
    logf = jnp.minimum(x, 0.0) - jnp.log(1.0 + jnp.exp(-jnp.abs(x)))
    tri = tri_ref[...]
    c = carry_ref[...]
    for part in _split3(logf):
        c = c + jnp.dot(tri, part, preferred_element_type=F32)
    carry_ref[...] = c[ts - 1:ts, :]
    parts = jnp.concatenate(list(_split3(c * LOG2E)) + [jnp.ones((ts, LANES), BF16)], axis=1)
    oq_ref[...] = jnp.dot(parts, pq_ref[...], preferred_element_type=F32).astype(BF16)
    ok_ref[...] = jnp.dot(parts, pk_ref[...], preferred_element_type=F32).astype(BF16)


def _fox_aug(fproj, b_f, *, batch, seq, hf, ts):
    T = fproj.shape[0]
    ns = seq // ts
    tri = np.tril(np.ones((ts, ts), np.float32))
    pq = np.zeros((4 * LANES, hf * LANES), np.float32)
    pk = np.zeros((4 * LANES, hf * LANES), np.float32)
    for h in range(hf):
        for p in range(3):
            pq[p * LANES + h, h * LANES + p] = 1.0
            pq[3 * LANES, h * LANES + 3 + p] = 1.0
            pk[3 * LANES, h * LANES + p] = 1.0
            pk[p * LANES + h, h * LANES + 3 + p] = -1.0
    bf = jnp.zeros((1, LANES), F32).at[0, :hf].set(b_f)
    out = jax.ShapeDtypeStruct((T, hf * LANES), BF16)
    return pl.pallas_call(
        functools.partial(_foxaug_kernel, ts=ts),
        grid=(batch, ns),
        in_specs=[
            pl.BlockSpec((ts, LANES), lambda b, s: (b * ns + s, 0)),
            pl.BlockSpec((1, LANES), lambda b, s: (0, 0)),
            pl.BlockSpec((ts, ts), lambda b, s: (0, 0)),
            pl.BlockSpec(pq.shape, lambda b, s: (0, 0)),
            pl.BlockSpec(pk.shape, lambda b, s: (0, 0)),
        ],
        out_specs=[pl.BlockSpec((ts, hf * LANES), lambda b, s: (b * ns + s, 0))] * 2,
        out_shape=[out, out],
        scratch_shapes=[pltpu.VMEM((1, LANES), F32)],
        compiler_params=_params("parallel", "arbitrary"),
        name="fox_aug",
    )(fproj, bf, jnp.asarray(tri, BF16), jnp.asarray(pq, BF16), jnp.asarray(pk, BF16))


def _flash_kernel(*refs, hs, tq, tk, mode, has_aug, gate_mode, gate_base, has_addin):
    it = iter(refs)
    q_ref = next(it)
    qa_ref = next(it) if has_aug else None
    k_ref = next(it)
    ka_ref = next(it) if has_aug else None
    v_ref = next(it)
    bias_ref = next(it)
    gate_ref = next(it) if gate_mode else None
    add_ref = next(it) if has_addin else None
    o_ref = next(it)
    q_scr = next(it)

    qi = pl.program_id(2)
    M = hs * tq
    for h in range(hs):
        q_scr[h * tq:(h + 1) * tq, :LANES] = q_ref[:, h * LANES:(h + 1) * LANES]
        if has_aug:
            q_scr[h * tq:(h + 1) * tq, LANES:] = qa_ref[...]
    ones = jnp.ones((tk, LANES), BF16)

    def scores(k0):
        k = k_ref[pl.ds(k0, tk), :]
        if has_aug:
            k = jnp.concatenate([k, ka_ref[pl.ds(k0, tk), :]], axis=1)
        return lax.dot_general(q_scr[...], k, _NT, preferred_element_type=F32)

    def biased(s, off):
        b = bias_ref[off]
        return (s.reshape(hs, tq, tk) + b[None]).reshape(M, tk) if hs > 1 else s + b

    def values(k0):
        return jnp.concatenate([v_ref[pl.ds(k0, tk), :], ones], axis=1)

    if mode == "window":
        back = tk - tq
        k0 = pl.multiple_of(jnp.maximum(qi * tq - back, 0), tq)
        s = biased(scores(k0), jnp.minimum(qi, back // tq))
        p = jnp.exp2(s - jnp.max(s, axis=1, keepdims=True))
        acc = jnp.dot(p.astype(BF16), values(k0), preferred_element_type=F32)
    else:
        m_scr, acc_scr, s_scr = next(it), next(it), next(it)
        m_scr[...] = jnp.full(m_scr.shape, NEG, F32)
        acc_scr[...] = jnp.zeros(acc_scr.shape, F32)

        def stage(kt):
            s_scr[kt & 1] = scores(pl.multiple_of(kt * tk, tk))

        def update(kt, off):
            s = s_scr[kt & 1]
            if off is not None:
                s = biased(s, off)
            m_prev = m_scr[...]
            m_new = jnp.maximum(m_prev, jnp.max(s, axis=1, keepdims=True))
            alpha = jnp.exp2(m_prev - m_new)
            p = jnp.exp2(s - jnp.concatenate([m_new] * (tk // LANES), axis=1))
            pv = jnp.dot(p.astype(BF16), values(pl.multiple_of(kt * tk, tk)),
                         preferred_element_type=F32)
            acc_scr[...] = jnp.concatenate([alpha, alpha], axis=1) * acc_scr[...] + pv
            m_scr[...] = m_new

        if mode == "causal":
            last = (qi * tq) // tk
            first, last_off = 0, qi - last * (tk // tq)
        else:
            last = qi
            first, last_off = jnp.maximum(qi - (bias_ref.shape[0] - 1), 0), 0

        stage(first)

        def body(kt, c):
            update(kt, None if mode == "causal" else qi - kt)
            stage(kt + 1)
            return c

        lax.fori_loop(first, last, body, 0)
        update(last, last_off)
        acc = acc_scr[...]

    o = acc[:, :LANES] * (1.0 / acc[:, LANES:])
    for h in range(hs):
        oh = o[h * tq:(h + 1) * tq, :]
        if gate_mode == "elem":
            oh = oh * gate_ref[...].astype(F32)
        elif gate_mode == "col":
            idx = (pl.program_id(1) * hs + h) * 3 + gate_base
            lane = lax.broadcasted_iota(jnp.int32, (tq, LANES), 1)
            oh = oh * jnp.sum(jnp.where(lane == idx, gate_ref[...], 0.0), axis=1, keepdims=True)
        sl = slice(h * LANES, (h + 1) * LANES)
        if has_addin:
            oh = oh + add_ref[:, sl]
        o_ref[:, sl] = oh.astype(o_ref.dtype)


def _flash(q_arr, q_col0, k_arr, k_col0, v_arr, v_col0, bias, *, n_kv, hs, tq, tk, mode, out_dtype,
           qaug=None, kaug=None, gate=None, gate_mode=None, gate_col0=0, gate_base=0, addin=None):
    B, S, _ = q_arr.shape
    qw = hs * LANES
    assert q_col0 % qw == 0 and k_col0 % LANES == 0 and v_col0 % LANES == 0
    assert S % tq == 0 and S % tk == 0 if mode != "window" else (S % tq == 0 and tk <= S)
    assert tk % tq == 0 and (mode != "band" or tq == tk) and bias.shape[1:] == (tq, tk)
    qb, kb, vb = q_col0 // qw, k_col0 // LANES, v_col0 // LANES
    assert (qaug is None) == (kaug is None)
    dk = LANES if qaug is None else 2 * LANES
    ins, specs = [q_arr], [pl.BlockSpec((None, tq, qw), lambda b, h, i: (b, i, qb + h))]
    if qaug is not None:
        ins.append(qaug)
        if qaug.ndim == 3:
            specs.append(pl.BlockSpec((None, tq, LANES), lambda b, h, i: (b, i, h)))
        else:
            specs.append(pl.BlockSpec((None, None, tq, LANES), lambda b, h, i: (b, h, i, 0)))
    ins.append(k_arr)
    specs.append(pl.BlockSpec((None, S, LANES), lambda b, h, i: (b, 0, kb + h)))
    if kaug is not None:
        ins.append(kaug)
        if kaug.ndim == 3:
            specs.append(pl.BlockSpec((None, S, LANES), lambda b, h, i: (b, 0, h)))
        else:
            specs.append(pl.BlockSpec((S, LANES), lambda b, h, i: (0, 0)))
    ins.append(v_arr)
    specs.append(pl.BlockSpec((None, S, LANES), lambda b, h, i: (b, 0, vb + h)))
    ins.append(bias)
    specs.append(pl.BlockSpec(bias.shape, lambda b, h, i: (0, 0, 0)))
    if gate_mode == "elem":
        gb = gate_col0 // LANES
        ins.append(gate)
        specs.append(pl.BlockSpec((None, tq, LANES), lambda b, h, i: (b, i, gb + h)))
    elif gate_mode == "col":
        ins.append(gate)
        specs.append(pl.BlockSpec((None, tq, LANES), lambda b, h, i: (b, i, 0)))
    if addin is not None:
        ins.append(addin)
        specs.append(pl.BlockSpec((None, tq, qw), lambda b, h, i: (b, i, h)))
    kern = functools.partial(
        _flash_kernel, hs=hs, tq=tq, tk=tk, mode=mode, has_aug=qaug is not None,
        gate_mode=gate_mode, gate_base=gate_base, has_addin=addin is not None)
    scratch = [pltpu.VMEM((hs * tq, dk), BF16)]
    if mode != "window":
        scratch += [pltpu.VMEM((hs * tq, LANES), F32), pltpu.VMEM((hs * tq, 2 * LANES), F32),
                    pltpu.VMEM((2, hs * tq, tk), F32)]
    return pl.pallas_call(
        kern,
        grid=(B, n_kv, S // tq),
        in_specs=specs,
        out_specs=pl.BlockSpec((None, tq, qw), lambda b, h, i: (b, i, h)),
        out_shape=jax.ShapeDtypeStruct((B, S, n_kv * qw), out_dtype),
        scratch_shapes=scratch,
        compiler_params=_params("parallel", "parallel", "arbitrary"),
        name="flash_attention",
    )(*ins)


def _bias_table(tq, tk, n, weight_of_distance):
    r = np.arange(tq)[:, None]
    c = np.arange(tk)[None, :]
    tabs = []
    for v in range(n):
        w = weight_of_distance(v * tq + r - c)
        tabs.append(np.where(w > 0, np.log2(np.maximum(w, 1.0)), NEG))
    return jnp.asarray(np.stack(tabs), F32)


def _dilated_weight(d):
    w = np.zeros(d.shape, np.float64)
    for window, dil in DILATED_PATTERNS:
        w += (d >= 0) & (d <= window) & (d % dil == 0)
    return w


def _compress_kernel(x_ref, pe_ref, w1_ref, w2_ref, g_ref, tab_ref, o_ref, *, nc):
    half = CMP_STRIDE * HEAD_DIM
    x = x_ref[...].astype(F32)
    top = (x + pe_ref[:, :half]).astype(BF16)
    bot = (x + pe_ref[:, half:]).astype(BF16)
    a = jnp.dot(top, w1_ref[:half, :], preferred_element_type=F32)
    b = jnp.dot(bot, w1_ref[half:, :], preferred_element_type=F32)
    pre = a + pltpu.roll(b, nc - 1, 0)
    hid = pre * (0.5 * (1.0 + jnp.tanh(np.sqrt(2.0 / np.pi) * (pre + 0.044715 * (pre * pre * pre)))))
    y = jnp.dot(hid.astype(BF16), w2_ref[...], preferred_element_type=F32)

    @pl.when(pl.program_id(0) == 0)
    def _():
        ms = jnp.mean(y * y, axis=-1, keepdims=True)
        o_ref[...] = _rope(y * lax.rsqrt(ms + EPS) * g_ref[...], tab_ref).astype(BF16)

    @pl.when(pl.program_id(0) != 0)
    def _():
        o_ref[...] = y.astype(BF16)


def _compress(x2, pe, w1, w2, g_kc, tabs_c):
    _, BG, nc, width = x2.shape
    return pl.pallas_call(
        functools.partial(_compress_kernel, nc=nc),
        grid=(2, BG),
        in_specs=[
            pl.BlockSpec((None, None, nc, width), lambda s, i: (s, i, 0, 0)),
            pl.BlockSpec((None, 1, 2 * width), lambda s, i: (s, 0, 0)),
            pl.BlockSpec((None, 2 * width, HEAD_DIM), lambda s, i: (s, 0, 0)),
            pl.BlockSpec((None, HEAD_DIM, HEAD_DIM), lambda s, i: (s, 0, 0)),
            pl.BlockSpec((1, HEAD_DIM), lambda s, i: (0, 0)),
            pl.BlockSpec((2, nc, LANES), lambda s, i: (0, 0, 0)),
        ],
        out_specs=pl.BlockSpec((None, None, nc, HEAD_DIM), lambda s, i: (s, i, 0, 0)),
        out_shape=jax.ShapeDtypeStruct((2, BG, nc, HEAD_DIM), BF16),
        compiler_params=_params("arbitrary", "arbitrary"),
        name="nsa_compress",
    )(x2, pe, w1, w2, g_kc.reshape(1, HEAD_DIM), tabs_c)


def _cmp_topk_kernel(q_ref, kc_ref, vc_ref, gate_ref, ov_ref, eye_ref, oc_ref, sel_ref, v_scr, *,
                     hs, t, nc, n_slc, n_sel):
    g = pl.program_id(1)
    q0 = pl.program_id(2) * t
    M = hs * t
    q = jnp.concatenate([q_ref[:, h * LANES:(h + 1) * LANES] for h in range(hs)], axis=0)
    s = lax.dot_general(q, kc_ref[...], _NT, preferred_element_type=F32)
    qpos = q0 + (lax.broadcasted_iota(jnp.int32, (M, nc), 0) & (t - 1))
    n = lax.broadcasted_iota(jnp.int32, (M, nc), 1)
    mask = n * CMP_STRIDE + (CMP_BLOCK - 1) <= qpos
    s = jnp.where(mask, s, NEG)
    m = jnp.max(s, axis=1, keepdims=True)
    e = jnp.where(mask, jnp.exp2(s - m), 0.0)
    p = e / jnp.maximum(jnp.sum(e, axis=1, keepdims=True), 1e-30)
    oc = jnp.dot(p.astype(BF16), vc_ref[...], preferred_element_type=F32)
    lane = lax.broadcasted_iota(jnp.int32, (t, LANES), 1)
    gates = gate_ref[...]
    psum = jnp.zeros((t, nc), F32)
    for h in range(hs):
        idx = (g * hs + h) * 3
        gcol = jnp.sum(jnp.where(lane == idx, gates, 0.0), axis=1, keepdims=True)
        oc_ref[:, h * LANES:(h + 1) * LANES] = oc[h * t:(h + 1) * t, :] * gcol
        psum = psum + p[h * t:(h + 1) * t, :]

    imp = jnp.zeros((LANES, t), F32)
    for part in _split3(psum):
        imp = imp + lax.dot_general(ov_ref[...], part, _NT, preferred_element_type=F32)
    jrow = lax.broadcasted_iota(jnp.int32, (LANES, t), 0)
    cur = (q0 + lax.broadcasted_iota(jnp.int32, (LANES, t), 1)) // SLC_BLOCK
    forced = (jrow == 0) | (jrow == cur) | (jrow == cur - 1)
    val = jnp.where(forced, FORCED_SCORE, imp)
    val = jnp.where(jrow <= cur, val, NEG)
    v_scr[...] = val

    n_grp = -(-n_slc // 8)
    sub8 = lax.broadcasted_iota(jnp.int32, (8, t), 0)
    vals = [val[8 * k:8 * k + 8, :] for k in range(n_grp)]
    ranks = [jnp.zeros((8, t), F32) for _ in range(n_grp)]
    for i in range(n_slc):
        vi = jnp.broadcast_to(v_scr[i:i + 1, :], (8, t))
        for k in range(n_grp):
            ge = jnp.where(vi >= vals[k], 1.0, 0.0)
            gt = jnp.where(vi > vals[k], 1.0, 0.0)
            if 8 * k > i:
                inc = ge
            elif 8 * k + 7 < i:
                inc = gt
            else:
                inc = jnp.where(sub8 > i - 8 * k, ge, gt)
            ranks[k] = ranks[k] + inc
    rank = jnp.concatenate(ranks + [jnp.zeros((LANES - 8 * n_grp, t), F32)], axis=0)
    sel = (rank < n_sel) & (jrow <= cur)
    selb = jnp.where(sel | (jrow >= n_slc), 0.0, NEG).astype(BF16)
    sel_ref[...] = lax.dot_general(eye_ref[...], selb, _NT, preferred_element_type=F32).astype(BF16)


def _cmp_topk(q_arr, kvc, gates, *, batch, seq, hs, t):
    G = N_KV_NSA
    nc = kvc.shape[2]
    n_slc = seq // SLC_BLOCK
    assert n_slc <= LANES and nc == seq // CMP_STRIDE
    qw = hs * LANES
    start = np.arange(nc) * CMP_STRIDE
    js = np.arange(LANES) * SLC_BLOCK
    ov = ((start[None, :] < js[:, None] + SLC_BLOCK) & (start[None, :] + CMP_BLOCK > js[:, None]))
    ov = ov & (np.arange(nc)[None, :] < nc - 1) & (np.arange(LANES)[:, None] < n_slc)
    kern = functools.partial(_cmp_topk_kernel, hs=hs, t=t, nc=nc, n_slc=n_slc,
                             n_sel=min(N_SELECT, n_slc))
    return pl.pallas_call(
        kern,
        grid=(batch, G, seq // t),
        in_specs=[
            pl.BlockSpec((None, t, qw), lambda b, g, i: (b, i, g)),
            pl.BlockSpec((None, None, nc, HEAD_DIM), lambda b, g, i: (0, b * G + g, 0, 0)),
            pl.BlockSpec((None, None, nc, HEAD_DIM), lambda b, g, i: (1, b * G + g, 0, 0)),
            pl.BlockSpec((None, t, LANES), lambda b, g, i: (b, i, 0)),
            pl.BlockSpec((LANES, nc), lambda b, g, i: (0, 0)),
            pl.BlockSpec((t, t), lambda b, g, i: (0, 0)),
        ],
        out_specs=[
            pl.BlockSpec((None, t, qw), lambda b, g, i: (b, i, g)),
            pl.BlockSpec((None, None, t, LANES), lambda b, g, i: (b, g, i, 0)),
        ],
        out_shape=[jax.ShapeDtypeStruct((batch, seq, G * qw), F32),
                   jax.ShapeDtypeStruct((batch, G, seq, LANES), BF16)],
        scratch_shapes=[pltpu.VMEM((LANES, t), F32)],
        compiler_params=_params("parallel", "parallel", "arbitrary"),
        name="nsa_cmp_topk",
    )(q_arr, kvc, kvc, gates, jnp.asarray(ov, BF16), jnp.eye(t, dtype=BF16))


def _rope_perm():
    half = ROPE_DIM // 2
    perm = np.arange(LANES)
    perm[half:ROPE_DIM] = np.arange(LANES // 2, LANES // 2 + half)
    perm[LANES // 2:LANES // 2 + half] = np.arange(half, ROPE_DIM)
    return perm


def _permute_heads(w):
    return w.reshape(w.shape[:-1] + (-1, LANES))[..., _rope_perm()].reshape(w.shape)


def _rope_tabs(pos):
    half = ROPE_DIM // 2
    inv = 1.0 / (ROPE_THETA ** (jnp.arange(0, ROPE_DIM, 2, dtype=F32) / ROPE_DIM))
    ang = pos.astype(F32)[:, None] * inv
    cos, sin = jnp.cos(ang), jnp.sin(ang)
    n = pos.shape[0]
    second = slice(LANES // 2, LANES // 2 + half)
    c = jnp.ones((n, LANES), F32).at[:, :half].set(cos).at[:, second].set(cos)
    s = jnp.zeros((n, LANES), F32).at[:, :half].set(-sin).at[:, second].set(sin)
    return jnp.stack([c, s])


def _pad_rows(g, rows=8):
    return jnp.zeros((rows, g.shape[-1]), F32).at[:g.shape[0]].set(g)


def _pad_cols(w, n):
    return jnp.zeros((w.shape[0], n), w.dtype).at[:, :w.shape[1]].set(w)


def kernel(x, ln_mix_g, ln_mlp_g, w_mlp_up, w_mlp_down, even_w_in, even_b_f, even_w_out,
           even_g_q_fox, even_g_k_fox, even_g_q_dil, even_g_k_dil, odd_w_in, odd_w_out,
           odd_phi_k_pe, odd_phi_k_w1, odd_phi_k_w2, odd_phi_v_pe, odd_phi_v_w1, odd_phi_v_w2,
           odd_g_q, odd_g_kc, odd_g_ks, odd_g_kw):
    B, S, D = x.shape
    T = B * S
    n_heads = D // HEAD_DIM
    hf = n_heads // 2
    hd = n_heads - hf
    fw, dw = hf * HEAD_DIM, hd * HEAD_DIM
    G = N_KV_NSA
    hpg = n_heads // G
    qw, kvw = n_heads * HEAD_DIM, G * HEAD_DIM
    depth = ln_mix_g.shape[0]

    tm = 512
    tm_proj = 1024
    t_fox = 512
    t_band = 256
    tk_sel = 512
    t_cmp = 256
    tabs = _rope_tabs(jnp.arange(S))
    nc = S // CMP_STRIDE
    tabs_c = _rope_tabs(jnp.arange(nc) * CMP_STRIDE + CMP_BLOCK - 1)

    causal_w = lambda d: (d >= 0).astype(np.float64)
    win_w = lambda d: ((d >= 0) & (d < WINDOW_NSA)).astype(np.float64)
    dil_span = max(w for w, _ in DILATED_PATTERNS)
    fox_bias = _bias_table(t_fox, t_fox, 1, causal_w)
    dil_bias = _bias_table(t_band, t_band, (dil_span + t_band - 1) // t_band + 1, _dilated_weight)
    sel_bias = _bias_table(t_band, tk_sel, tk_sel // t_band, causal_w)
    win_back = -(-(WINDOW_NSA - 1) // t_band) * t_band
    win_bias = _bias_table(t_band, win_back + t_band, win_back // t_band + 1, win_w)
    onehot_blk = jnp.asarray(
        (np.arange(S)[:, None] // SLC_BLOCK == np.arange(LANES)[None, :]).astype(np.float32), BF16)

    h = x.reshape(T, D)
    for layer in range(depth):
        i = layer // 2
        if layer % 2 == 0:
            w_in = even_w_in[i]
            c_qd = 4 * fw + hf
            w_main = jnp.concatenate(
                [w_in[:, :4 * fw], _permute_heads(w_in[:, c_qd:c_qd + 2 * dw]),
                 w_in[:, c_qd + 2 * dw:]], axis=1).astype(BF16)
            w_f = _pad_cols(w_in[:, 4 * fw:4 * fw + hf], LANES).astype(BF16)
            gains = _pad_rows(jnp.stack([even_g_q_fox[i], even_g_k_fox[i],
                                         _permute_heads(even_g_q_dil[i]),
                                         _permute_heads(even_g_k_dil[i])]))
            segs = [Seg(fw, norm=True, gain=0, scale=QSCALE), Seg(fw, norm=True, gain=1), Seg(fw),
                    Seg(fw, sigmoid=True),
                    Seg(dw, norm=True, gain=2, rope=True, scale=QSCALE),
                    Seg(dw, norm=True, gain=3, rope=True), Seg(dw)]
            proj = _norm_proj(h, ln_mix_g[layer], w_main, gains, tabs, segs, seq=S, tm=tm_proj,
                              tn=min(1024, fw, dw), out_dtype=BF16).reshape(B, S, -1)
            fproj = _norm_proj(h, ln_mix_g[layer], w_f, gains, tabs, [Seg(LANES)], seq=S, tm=tm,
                               tn=LANES, out_dtype=F32)
            qaug, kaug = _fox_aug(fproj, even_b_f[i], batch=B, seq=S, hf=hf, ts=512)
            o_a = _flash(proj, 0, proj, fw, proj, 2 * fw, fox_bias, n_kv=hf, hs=1, tq=t_fox,
                         tk=t_fox, mode="causal", out_dtype=BF16, qaug=qaug.reshape(B, S, -1),
                         kaug=kaug.reshape(B, S, -1), gate=proj, gate_mode="elem",
                         gate_col0=3 * fw)
            o_b = _flash(proj, 4 * fw, proj, 4 * fw + dw, proj, 4 * fw + 2 * dw, dil_bias, n_kv=hd,
                         hs=1, tq=t_band, tk=t_band, mode="band", out_dtype=BF16)
            w_out = even_w_out[i].astype(BF16)
            h = _out_proj([o_a.reshape(T, fw), o_b.reshape(T, dw)], [w_out[:fw], w_out[fw:]], h,
                          tm=tm_proj, tn=1024)
        else:
            w_in = odd_w_in[i]
            n_main = qw + 6 * kvw
            c_kc, c_vc, c_ks, c_kw, c_vs, c_vw = (qw + n * kvw for n in range(6))
            w_main = jnp.concatenate(
                [_permute_heads(w_in[:, :qw]), w_in[:, qw:qw + 2 * kvw],
                 _permute_heads(w_in[:, qw + 2 * kvw:qw + 3 * kvw]),
                 _permute_heads(w_in[:, qw + 4 * kvw:qw + 5 * kvw]),
                 w_in[:, qw + 3 * kvw:qw + 4 * kvw], w_in[:, qw + 5 * kvw:n_main]],
                axis=1).astype(BF16)
            w_g = _pad_cols(w_in[:, n_main:], LANES).astype(BF16)
            gains = _pad_rows(_permute_heads(jnp.stack([odd_g_q[i], odd_g_ks[i], odd_g_kw[i]])))
            segs = [Seg(qw, norm=True, gain=0, rope=True, scale=QSCALE), Seg(2 * kvw),
                    Seg(2 * kvw, norm=True, gain=1, n_gains=2, rope=True), Seg(2 * kvw)]
            proj = _norm_proj(h, ln_mix_g[layer], w_main, gains, tabs, segs, seq=S, tm=tm_proj,
                              tn=min(1024, 2 * kvw), out_dtype=BF16).reshape(B, S, -1)
            gates = _norm_proj(h, ln_mix_g[layer], w_g, gains, tabs, [Seg(LANES, sigmoid=True)],
                               seq=S, tm=tm, tn=LANES, out_dtype=F32).reshape(B, S, LANES)

            def chunks(col0):
                xt = proj[:, :, col0:col0 + kvw].reshape(B, S, G, HEAD_DIM).transpose(0, 2, 1, 3)
                return xt.reshape(B * G, nc, CMP_STRIDE * HEAD_DIM)

            x2 = jnp.stack([chunks(c_kc), chunks(c_vc)])
            pe = jnp.stack([odd_phi_k_pe[i], odd_phi_v_pe[i]]).reshape(2, 1, CMP_BLOCK * HEAD_DIM)
            w1 = jnp.stack([odd_phi_k_w1[i], odd_phi_v_w1[i]]).astype(BF16)
            w2 = jnp.stack([_permute_heads(odd_phi_k_w2[i]), odd_phi_v_w2[i]]).astype(BF16)
            kvc = _compress(x2, pe, w1, w2, _permute_heads(odd_g_kc[i]), tabs_c)
            o_c, selb = _cmp_topk(proj, kvc, gates, batch=B, seq=S, hs=hpg, t=t_cmp)
            o_cs = _flash(proj, 0, proj, c_ks, proj, c_vs, sel_bias, n_kv=G,
                          hs=hpg, tq=t_band, tk=tk_sel, mode="causal", out_dtype=F32, qaug=selb,
                          kaug=onehot_blk, gate=gates, gate_mode="col", gate_base=1, addin=o_c)
            o = _flash(proj, 0, proj, c_kw, proj, c_vw, win_bias, n_kv=G, hs=hpg,
                       tq=t_band, tk=win_back + t_band, mode="window", out_dtype=BF16, gate=gates,
                       gate_mode="col", gate_base=2, addin=o_cs)
            h = _out_proj([o.reshape(T, qw)], [odd_w_out[i].astype(BF16)], h, tm=tm_proj, tn=1024)
        h = _mlp(h, ln_mlp_g[layer], w_mlp_up[layer].astype(BF16), w_mlp_down[layer].astype(BF16),
                 tm=tm, tf=1024)
    return h.reshape(B, S, D)
```

```python
import functools
from typing import NamedTuple

import numpy as np
import jax
import jax.numpy as jnp
from jax import lax
from jax.experimental import pallas as pl
from jax.experimental.pallas import tpu as pltpu

HEAD_DIM = 128
LANES = 128
N_KV_NSA = 4
DILATED_PATTERNS = ((128, 1), (512, 4), (2048, 16))
CMP_BLOCK = 32
CMP_STRIDE = 16
SLC_BLOCK = 64
N_SELECT = 16
WINDOW_NSA = 512
ROPE_THETA = 500000.0
ROPE_DIM = HEAD_DIM // 4
EPS = 1e-6
NEG = -1e30
FORCED_SCORE = 1e9
SCALE = HEAD_DIM ** -0.5
LOG2E = float(np.log2(np.e))
QSCALE = SCALE * LOG2E
VMEM_LIMIT = 56 * 1024 * 1024

BF16 = jnp.bfloat16
F32 = jnp.float32
_NT = (((1,), (1,)), ((), ()))


def _params(*sem):
    return pltpu.CompilerParams(dimension_semantics=sem, vmem_limit_bytes=VMEM_LIMIT)


def _split3(x):
    hi = x.astype(BF16)
    r1 = x - hi.astype(F32)
    mid = r1.astype(BF16)
    lo = (r1 - mid.astype(F32)).astype(BF16)
    return hi, mid, lo


class Seg(NamedTuple):
    width: int
    norm: bool = False
    gain: int = 0
    n_gains: int = 1
    rope: bool = False
    scale: float = 1.0
    sigmoid: bool = False
    chunks: bool = False


def _rope(y, tab_ref):
    return y * tab_ref[0] + pltpu.roll(y, LANES // 2, 1) * tab_ref[1]


def _proj_kernel(*refs, segs, tn, gate_sigmoid):
    x_ref, g_ref, w_ref, wg_ref, gains_ref, tab_ref, o_ref, og_ref = refs[:8]
    oc_ref = refs[8] if any(s.chunks for s in segs) else None
    xn_ref, y_scr, r_scr = refs[-3:]
    j = pl.program_id(1)

    @pl.when(j == 0)
    def _():
        x = x_ref[...]
        ms = jnp.mean(x * x, axis=-1, keepdims=True)
        xn_ref[...] = (x * lax.rsqrt(ms + EPS) * g_ref[...]).astype(BF16)
        gate = jnp.dot(xn_ref[...], wg_ref[...], preferred_element_type=F32)
        og_ref[...] = 1.0 / (1.0 + jnp.exp(-gate)) if gate_sigmoid else gate

    sub = min(tn, 2 * LANES)
    lo = 0
    for seg in segs:
        nt = seg.width // tn

        @pl.when((j >= lo) & (j < lo + nt))
        def _(seg=seg, lo=lo):
            heads = [slice(hd * LANES, (hd + 1) * LANES) for hd in range(sub // LANES)]
            for c in range(tn // sub):
                acc = jnp.dot(xn_ref[...], w_ref[:, c * sub:(c + 1) * sub],
                              preferred_element_type=F32)
                if not seg.norm:
                    y = 1.0 / (1.0 + jnp.exp(-acc)) if seg.sigmoid else acc
                    o_ref[:, c * sub:(c + 1) * sub] = y.astype(o_ref.dtype)
                    if seg.chunks:
                        for hd, sl in enumerate(heads):
                            head = (c * sub) // LANES + hd
                            oc_ref[head // N_KV_NSA, head % N_KV_NSA] = y[:, sl].astype(oc_ref.dtype)
                    continue
                ys, rs = y_scr.at[c % 2], r_scr.at[c % 2]
                ys[...] = acc
                for sl in heads:
                    y = ys[:, sl]
                    r = lax.rsqrt(jnp.mean(y * y, axis=-1, keepdims=True) + EPS)
                    rs[:, sl] = jnp.broadcast_to(r, y.shape)
                for hd, sl in enumerate(heads):
                    gi = seg.gain
                    if seg.n_gains > 1:
                        tile = 0 if seg.width == tn else j - lo
                        head = tile * (tn // LANES) + (c * sub) // LANES + hd
                        gi = gi + head // (seg.width // LANES // seg.n_gains)
                    ys[:, sl] = ys[:, sl] * rs[:, sl] * gains_ref[pl.ds(gi, 1), :]
                for hd, sl in enumerate(heads):
                    y = _rope(ys[:, sl], tab_ref) if seg.rope else ys[:, sl]
                    if seg.scale != 1.0:
                        y = y * seg.scale
                    col = c * sub + hd * LANES
                    o_ref[:, col:col + LANES] = y.astype(o_ref.dtype)

        lo += nt


def _norm_proj(x, g, w, w_gate, gains, tabs, segs, *, seq, tm, tn, gate_sigmoid):
    T, D = x.shape
    N = w.shape[1]
    assert sum(s.width for s in segs) == N and all(s.width % tn == 0 for s in segs)
    assert T % tm == 0 and seq % tm == 0 and tn % LANES == 0
    assert all(s.norm or (not s.rope and s.scale == 1.0) for s in segs)
    assert not any(s.norm and (s.sigmoid or s.chunks) for s in segs)
    assert all(s.width == tn == 2 * N_KV_NSA * LANES for s in segs if s.chunks)
    nseq = seq // tm
    sub = min(tn, 2 * LANES)
    out_specs = [pl.BlockSpec((tm, tn), lambda i, j: (i, j)),
                 pl.BlockSpec((tm, LANES), lambda i, j: (i, 0))]
    out_shape = [jax.ShapeDtypeStruct((T, N), BF16), jax.ShapeDtypeStruct((T, LANES), F32)]
    if any(s.chunks for s in segs):
        out_specs.append(pl.BlockSpec((2, None, N_KV_NSA, tm, LANES),
                                      lambda i, j: (0, i // nseq, 0, i % nseq, 0)))
        out_shape.append(jax.ShapeDtypeStruct((2, T // seq, N_KV_NSA, seq, LANES), BF16))
    return pl.pallas_call(
        functools.partial(_proj_kernel, segs=tuple(segs), tn=tn, gate_sigmoid=gate_sigmoid),
        grid=(T // tm, N // tn),
        in_specs=[
            pl.BlockSpec((tm, D), lambda i, j: (i, 0)),
            pl.BlockSpec((1, D), lambda i, j: (0, 0)),
            pl.BlockSpec((D, tn), lambda i, j: (0, j)),
            pl.BlockSpec((D, LANES), lambda i, j: (0, 0)),
            pl.BlockSpec(gains.shape, lambda i, j: (0, 0)),
            pl.BlockSpec((2, tm, LANES), lambda i, j: (0, i % nseq, 0)),
        ],
        out_specs=out_specs,
        out_shape=out_shape,
        scratch_shapes=[pltpu.VMEM((tm, D), BF16), pltpu.VMEM((2, tm, sub), F32),
                        pltpu.VMEM((2, tm, sub), F32)],
        compiler_params=_params("parallel", "arbitrary"),
        name="norm_proj",
    )(x, g.reshape(1, D), w, w_gate, gains, tabs)


def _outproj_kernel(*refs, n_parts):
    a_refs = refs[:n_parts]
    w_refs = refs[n_parts:2 * n_parts]
    h_ref = refs[2 * n_parts]
    o_ref = refs[2 * n_parts + 1]
    acc = h_ref[...]
    for a_ref, w_ref in zip(a_refs, w_refs):
        acc = acc + jnp.dot(a_ref[...], w_ref[...], preferred_element_type=F32)
    o_ref[...] = acc


def _out_proj(parts, weights, h, *, tm, tn):
    T, D = h.shape
    n = len(parts)
    in_specs = [pl.BlockSpec((tm, p.shape[1]), lambda i, j: (i, 0)) for p in parts]
    in_specs += [pl.BlockSpec((w.shape[0], tn), lambda i, j: (0, j)) for w in weights]
    in_specs += [pl.BlockSpec((tm, tn), lambda i, j: (i, j))]
    return pl.pallas_call(
        functools.partial(_outproj_kernel, n_parts=n),
        grid=(T // tm, D // tn),
        in_specs=in_specs,
        out_specs=pl.BlockSpec((tm, tn), lambda i, j: (i, j)),
        out_shape=jax.ShapeDtypeStruct((T, D), F32),
        compiler_params=_params("parallel", "arbitrary"),
        name="out_proj",
    )(*parts, *weights, h)


def _mlp_kernel(h_ref, g_ref, wu_ref, wd_ref, o_ref, xn_ref):
    f = pl.program_id(1)

    @pl.when(f == 0)
    def _():
        x = h_ref[...]
        ms = jnp.mean(x * x, axis=-1, keepdims=True)
        xn_ref[...] = (x * lax.rsqrt(ms + EPS) * g_ref[...]).astype(BF16)
        o_ref[...] = x

    tf = wu_ref.shape[1]
    sub = min(tf, 4 * LANES)
    total = None
    for c in range(tf // sub):
        u = jnp.dot(xn_ref[...], wu_ref[:, c * sub:(c + 1) * sub], preferred_element_type=F32)
        a = jnp.square(jnp.maximum(u, 0.0)).astype(BF16)
        part = jnp.dot(a, wd_ref[c * sub:(c + 1) * sub, :], preferred_element_type=F32)
        total = part if total is None else total + part
    o_ref[...] += total


def _mlp(h, g, w_up, w_down, *, tm, tf):
    T, D = h.shape
    FF = w_up.shape[1]
    return pl.pallas_call(
        _mlp_kernel,
        grid=(T // tm, FF // tf),
        in_specs=[
            pl.BlockSpec((tm, D), lambda i, f: (i, 0)),
            pl.BlockSpec((1, D), lambda i, f: (0, 0)),
            pl.BlockSpec((D, tf), lambda i, f: (0, f)),
            pl.BlockSpec((tf, D), lambda i, f: (f, 0)),
        ],
        out_specs=pl.BlockSpec((tm, D), lambda i, f: (i, 0)),
        out_shape=jax.ShapeDtypeStruct((T, D), F32),
        scratch_shapes=[pltpu.VMEM((tm, D), BF16)],
        compiler_params=_params("parallel", "arbitrary"),
        name="sq_relu_mlp",
    )(h, g.reshape(1, D), w_up, w_down)


def _foxaug_kernel(f_ref, bf_ref, tri_ref, pq_ref, pk_ref, oq_ref, ok_ref, carry_ref, *, ts):
    @pl.when(pl.program_id(1) == 0)
    def _():
        carry_ref[...] = jnp.zeros_like(carry_ref)

    x = f_ref[...] + bf_ref[...]
    logf = jnp.minimum(x, 0.0) - jnp.log(1.0 + jnp.exp(-jnp.abs(x)))
    tri = tri_ref[...]
    c = carry_ref[...]
    for part in _split3(logf):
        c = c + jnp.dot(tri, part, preferred_element_type=F32)
    carry_ref[...] = c[ts - 1:ts, :]
    parts = jnp.concatenate(list(_split3(c * LOG2E)) + [jnp.ones((ts, LANES), BF16)], axis=1)
    oq_ref[...] = jnp.dot(parts, pq_ref[...], preferred_element_type=F32).astype(BF16)
    ok_ref[...] = jnp.dot(parts, pk_ref[...], preferred_element_type=F32).astype(BF16)


def _fox_aug(fproj, b_f, *, batch, seq, hf, ts):
    T = fproj.shape[0]
    ns = seq // ts
    tri = np.tril(np.ones((ts, ts), np.float32))
    pq = np.zeros((4 * LANES, hf * LANES), np.float32)
    pk = np.zeros((4 * LANES, hf * LANES), np.float32)
    for h in range(hf):
        for p in range(3):
            pq[p * LANES + h, h * LANES + p] = 1.0
            pq[3 * LANES, h * LANES + 3 + p] = 1.0
            pk[3 * LANES, h * LANES + p] = 1.0
            pk[p * LANES + h, h * LANES + 3 + p] = -1.0
    bf = jnp.zeros((1, LANES), F32).at[0, :hf].set(b_f)
    out = jax.ShapeDtypeStruct((T, hf * LANES), BF16)
    return pl.pallas_call(
        functools.partial(_foxaug_kernel, ts=ts),
        grid=(batch, ns),
        in_specs=[
            pl.BlockSpec((ts, LANES), lambda b, s: (b * ns + s, 0)),
            pl.BlockSpec((1, LANES), lambda b, s: (0, 0)),
            pl.BlockSpec((ts, ts), lambda b, s: (0, 0)),
            pl.BlockSpec(pq.shape, lambda b, s: (0, 0)),
            pl.BlockSpec(pk.shape, lambda b, s: (0, 0)),
        ],
        out_specs=[pl.BlockSpec((ts, hf * LANES), lambda b, s: (b * ns + s, 0))] * 2,
        out_shape=[out, out],
        scratch_shapes=[pltpu.VMEM((1, LANES), F32)],
        compiler_params=_params("parallel", "arbitrary"),
        name="fox_aug",
    )(fproj, bf, jnp.asarray(tri, BF16), jnp.asarray(pq, BF16), jnp.asarray(pk, BF16))


def _flash_kernel(*refs, nk, split, hs, tq, tk, mode, has_aug, gate_mode, gate_base, has_addin):
    it = iter(refs)
    q_ref = next(it)
    qa_ref = next(it) if has_aug else None
    k_ref = next(it)
    ka_ref = next(it) if has_aug else None
    v_ref = next(it)
    bias_ref = next(it)
    gate_ref = next(it) if gate_mode else None
    add_ref = next(it) if has_addin else None
    o_ref = next(it)
    scr = list(it)
    per = 1 if mode == "window" else 4
    nch = nk * split
    hc = hs // split
    chains = [scr[c * per:(c + 1) * per] for c in range(nch)]

    qi = pl.program_id(2)
    M = hc * tq
    ones = jnp.ones((tk, LANES), BF16)
    lanes = lambda c: slice((c // split) * LANES, (c // split + 1) * LANES)
    head0 = lambda c: (c // split) * hs + (c % split) * hc

    for c in range(nch):
        q_scr = chains[c][0]
        for h in range(hc):
            col = (head0(c) + h) * LANES
            q_scr[h * tq:(h + 1) * tq, :LANES] = q_ref[:, col:col + LANES]
            if has_aug:
                q_scr[h * tq:(h + 1) * tq, LANES:] = qa_ref[:, lanes(c)]

    def scores(c, k0):
        k = k_ref[pl.ds(k0, tk), lanes(c)]
        if has_aug:
            k = jnp.concatenate([k, ka_ref[pl.ds(k0, tk), lanes(c)]], axis=1)
        return lax.dot_general(chains[c][0][...], k, _NT, preferred_element_type=F32)

    def biased(s, off):
        b = bias_ref[off]
        return (s.reshape(hc, tq, tk) + b[None]).reshape(M, tk) if hc > 1 else s + b

    def values(c, k0):
        return jnp.concatenate([v_ref[pl.ds(k0, tk), lanes(c)], ones], axis=1)

    accs = []
    if mode == "window":
        back = tk - tq
        k0 = pl.multiple_of(jnp.maximum(qi * tq - back, 0), tq)
        for c in range(nch):
            s = biased(scores(c, k0), jnp.minimum(qi, back // tq))
            p = jnp.exp2(s - jnp.max(s, axis=1, keepdims=True))
            accs.append(jnp.dot(p.astype(BF16), values(c, k0), preferred_element_type=F32))
    else:
        for c in range(nch):
            _, m_scr, acc_scr, _ = chains[c]
            m_scr[...] = jnp.full(m_scr.shape, NEG, F32)
            acc_scr[...] = jnp.zeros(acc_scr.shape, F32)

        def stage(c, kt):
            chains[c][3][kt & 1] = scores(c, pl.multiple_of(kt * tk, tk))

        def update(c, kt, off):
            _, m_scr, acc_scr, s_scr = chains[c]
            s = s_scr[kt & 1]
            if off is not None:
                s = biased(s, off)
            m_prev = m_scr[...]
            m_new = jnp.maximum(m_prev, jnp.max(s, axis=1, keepdims=True))
            alpha = jnp.exp2(m_prev - m_new)
            p = jnp.exp2(s - jnp.concatenate([m_new] * (tk // LANES), axis=1))
            pv = jnp.dot(p.astype(BF16), values(c, pl.multiple_of(kt * tk, tk)),
                         preferred_element_type=F32)
            acc_scr[...] = jnp.concatenate([alpha, alpha], axis=1) * acc_scr[...] + pv
            m_scr[...] = m_new

        if mode == "causal":
            last = (qi * tq) // tk
            first, last_off = 0, qi - last * (tk // tq)
        else:
            last = qi
            first, last_off = jnp.maximum(qi - (bias_ref.shape[0] - 1), 0), 0

        for c in range(nch):
            stage(c, first)

        def body(kt, carry):
            for c in range(nch):
                update(c, kt, None if mode == "causal" else qi - kt)
                stage(c, kt + 1)
            return carry

        lax.fori_loop(first, last, body, 0)
        for c in range(nch):
            update(c, last, last_off)
            accs.append(chains[c][2][...])

    for c in range(nch):
        o = accs[c][:, :LANES] * (1.0 / accs[c][:, LANES:])
        for h in range(hc):
            oh = o[h * tq:(h + 1) * tq, :]
            head = head0(c) + h
            sl = slice(head * LANES, (head + 1) * LANES)
            if gate_mode == "elem":
                oh = oh * gate_ref[:, sl].astype(F32)
            elif gate_mode == "col":
                idx = (pl.program_id(1) * hs + head) * 3 + gate_base
                lane = lax.broadcasted_iota(jnp.int32, (tq, LANES), 1)
                oh = oh * jnp.sum(jnp.where(lane == idx, gate_ref[...], 0.0), axis=1, keepdims=True)
            if has_addin:
                oh = oh + add_ref[:, sl]
            o_ref[:, sl] = oh.astype(o_ref.dtype)


def _flash(q_arr, q_col0, k_arr, k_col0, v_arr, v_col0, bias, *, n_kv, hs, tq, tk, mode, out_dtype,
           nk=1, split=1, qaug=None, kaug=None, gate=None, gate_mode=None, gate_col0=0, gate_base=0,
           addin=None):
    B, S, _ = q_arr.shape
    qw = hs * LANES
    assert q_col0 % (nk * qw) == 0 and k_col0 % (nk * LANES) == 0 and v_col0 % (nk * LANES) == 0
    assert S % tq == 0 and S % tk == 0 if mode != "window" else (S % tq == 0 and tk <= S)
    assert tk % tq == 0 and (mode != "band" or tq == tk) and bias.shape[1:] == (tq, tk)
    assert n_kv % nk == 0 and hs % split == 0
    assert nk == 1 or (gate_mode != "col" and (qaug is None or qaug.ndim == 3))
    qb, kb, vb = q_col0 // (nk * qw), k_col0 // (nk * LANES), v_col0 // (nk * LANES)
    assert (qaug is None) == (kaug is None)
    dk = LANES if qaug is None else 2 * LANES
    ins, specs = [q_arr], [pl.BlockSpec((None, tq, nk * qw), lambda b, h, i: (b, i, qb + h))]
    if qaug is not None:
        ins.append(qaug)
        if qaug.ndim == 3:
            specs.append(pl.BlockSpec((None, tq, nk * LANES), lambda b, h, i: (b, i, h)))
        else:
            specs.append(pl.BlockSpec((None, None, tq, LANES), lambda b, h, i: (b, h, i, 0)))
    ins.append(k_arr)
    specs.append(pl.BlockSpec((None, S, nk * LANES), lambda b, h, i: (b, 0, kb + h)))
    if kaug is not None:
        ins.append(kaug)
        if kaug.ndim == 3:
            specs.append(pl.BlockSpec((None, S, nk * LANES), lambda b, h, i: (b, 0, h)))
        else:
            specs.append(pl.BlockSpec((S, LANES), lambda b, h, i: (0, 0)))
    ins.append(v_arr)
    specs.append(pl.BlockSpec((None, S, nk * LANES), lambda b, h, i: (b, 0, vb + h)))
    ins.append(bias)
    specs.append(pl.BlockSpec(bias.shape, lambda b, h, i: (0, 0, 0)))
    if gate_mode == "elem":
        assert gate_col0 % (nk * LANES) == 0
        gb = gate_col0 // (nk * LANES)
        ins.append(gate)
        specs.append(pl.BlockSpec((None, tq, nk * LANES), lambda b, h, i: (b, i, gb + h)))
    elif gate_mode == "col":
        ins.append(gate)
        specs.append(pl.BlockSpec((None, tq, LANES), lambda b, h, i: (b, i, 0)))
    if addin is not None:
        ins.append(addin)
        specs.append(pl.BlockSpec((None, tq, nk * qw), lambda b, h, i: (b, i, h)))
    kern = functools.partial(
        _flash_kernel, nk=nk, split=split, hs=hs, tq=tq, tk=tk, mode=mode, has_aug=qaug is not None,
        gate_mode=gate_mode, gate_base=gate_base, has_addin=addin is not None)
    mc = hs // split * tq
    scratch = []
    for _ in range(nk * split):
        scratch.append(pltpu.VMEM((mc, dk), BF16))
        if mode != "window":
            scratch += [pltpu.VMEM((mc, LANES), F32), pltpu.VMEM((mc, 2 * LANES), F32),
                        pltpu.VMEM((2, mc, tk), F32)]
    return pl.pallas_call(
        kern,
        grid=(B, n_kv // nk, S // tq),
        in_specs=specs,
        out_specs=pl.BlockSpec((None, tq, nk * qw), lambda b, h, i: (b, i, h)),
        out_shape=jax.ShapeDtypeStruct((B, S, n_kv * qw), out_dtype),
        scratch_shapes=scratch,
        compiler_params=_params("parallel", "parallel", "arbitrary"),
        name="flash_attention",
    )(*ins)


def _bias_table(tq, tk, n, weight_of_distance):
    r = np.arange(tq)[:, None]
    c = np.arange(tk)[None, :]
    tabs = []
    for v in range(n):
        w = weight_of_distance(v * tq + r - c)
        tabs.append(np.where(w > 0, np.log2(np.maximum(w, 1.0)), NEG))
    return jnp.asarray(np.stack(tabs), F32)


def _dilated_weight(d):
    w = np.zeros(d.shape, np.float64)
    for window, dil in DILATED_PATTERNS:
        w += (d >= 0) & (d <= window) & (d % dil == 0)
    return w


def _compress_kernel(x_ref, pe_ref, w1_ref, w2_ref, g_ref, tab_ref, o_ref, *, nc):
    half = CMP_STRIDE * HEAD_DIM
    x = x_ref[...].astype(F32)
    top = (x + pe_ref[:, :half]).astype(BF16)
    bot = (x + pe_ref[:, half:]).astype(BF16)
    a = jnp.dot(top, w1_ref[:half, :], preferred_element_type=F32)
    b = jnp.dot(bot, w1_ref[half:, :], preferred_element_type=F32)
    pre = a + pltpu.roll(b, nc - 1, 0)
    hid = pre * (0.5 * (1.0 + jnp.tanh(np.sqrt(2.0 / np.pi) * (pre + 0.044715 * (pre * pre * pre)))))
    y = jnp.dot(hid.astype(BF16), w2_ref[...], preferred_element_type=F32)

    @pl.when(pl.program_id(0) == 0)
    def _():
        ms = jnp.mean(y * y, axis=-1, keepdims=True)
        o_ref[...] = _rope(y * lax.rsqrt(ms + EPS) * g_ref[...], tab_ref).astype(BF16)

    @pl.when(pl.program_id(0) != 0)
    def _():
        o_ref[...] = y.astype(BF16)


def _compress(x2, pe, w1, w2, g_kc, tabs_c):
    _, BG, nc, width = x2.shape
    return pl.pallas_call(
        functools.partial(_compress_kernel, nc=nc),
        grid=(2, BG),
        in_specs=[
            pl.BlockSpec((None, None, nc, width), lambda s, i: (s, i, 0, 0)),
            pl.BlockSpec((None, 1, 2 * width), lambda s, i: (s, 0, 0)),
            pl.BlockSpec((None, 2 * width, HEAD_DIM), lambda s, i: (s, 0, 0)),
            pl.BlockSpec((None, HEAD_DIM, HEAD_DIM), lambda s, i: (s, 0, 0)),
            pl.BlockSpec((1, HEAD_DIM), lambda s, i: (0, 0)),
            pl.BlockSpec((2, nc, LANES), lambda s, i: (0, 0, 0)),
        ],
        out_specs=pl.BlockSpec((None, None, nc, HEAD_DIM), lambda s, i: (s, i, 0, 0)),
        out_shape=jax.ShapeDtypeStruct((2, BG, nc, HEAD_DIM), BF16),
        compiler_params=_params("arbitrary", "arbitrary"),
        name="nsa_compress",
    )(x2, pe, w1, w2, g_kc.reshape(1, HEAD_DIM), tabs_c)


def _cmp_topk_kernel(q_ref, kc_ref, vc_ref, gate_ref, ov_ref, eye_ref, oc_ref, sel_ref, v_scr, *,
                     hs, t, nc, n_slc, n_sel):
    g = pl.program_id(1)
    q0 = pl.program_id(2) * t
    M = hs * t
    q = jnp.concatenate([q_ref[:, h * LANES:(h + 1) * LANES] for h in range(hs)], axis=0)
    s = lax.dot_general(q, kc_ref[...], _NT, preferred_element_type=F32)
    qpos = q0 + (lax.broadcasted_iota(jnp.int32, (M, nc), 0) & (t - 1))
    n = lax.broadcasted_iota(jnp.int32, (M, nc), 1)
    mask = n * CMP_STRIDE + (CMP_BLOCK - 1) <= qpos
    s = jnp.where(mask, s, NEG)
    m = jnp.max(s, axis=1, keepdims=True)
    e = jnp.where(mask, jnp.exp2(s - m), 0.0)
    p = e / jnp.maximum(jnp.sum(e, axis=1, keepdims=True), 1e-30)
    oc = jnp.dot(p.astype(BF16), vc_ref[...], preferred_element_type=F32)
    lane = lax.broadcasted_iota(jnp.int32, (t, LANES), 1)
    gates = gate_ref[...]
    psum = jnp.zeros((t, nc), F32)
    for h in range(hs):
        idx = (g * hs + h) * 3
        gcol = jnp.sum(jnp.where(lane == idx, gates, 0.0), axis=1, keepdims=True)
        oc_ref[:, h * LANES:(h + 1) * LANES] = oc[h * t:(h + 1) * t, :] * gcol
        psum = psum + p[h * t:(h + 1) * t, :]

    imp = jnp.zeros((LANES, t), F32)
    for part in _split3(psum):
        imp = imp + lax.dot_general(ov_ref[...], part, _NT, preferred_element_type=F32)
    jrow = lax.broadcasted_iota(jnp.int32, (LANES, t), 0)
    cur = (q0 + lax.broadcasted_iota(jnp.int32, (LANES, t), 1)) // SLC_BLOCK
    forced = (jrow == 0) | (jrow == cur) | (jrow == cur - 1)
    val = jnp.where(forced, FORCED_SCORE, imp)
    val = jnp.where(jrow <= cur, val, NEG)
    v_scr[...] = val

    n_grp = -(-n_slc // 8)
    sub8 = lax.broadcasted_iota(jnp.int32, (8, t), 0)
    vals = [val[8 * k:8 * k + 8, :] for k in range(n_grp)]
    ranks = [jnp.zeros((8, t), F32) for _ in range(n_grp)]
    for i in range(n_slc):
        vi = jnp.broadcast_to(v_scr[i:i + 1, :], (8, t))
        for k in range(n_grp):
            ge = jnp.where(vi >= vals[k], 1.0, 0.0)
            gt = jnp.where(vi > vals[k], 1.0, 0.0)
            if 8 * k > i:
                inc = ge
            elif 8 * k + 7 < i:
                inc = gt
            else:
                inc = jnp.where(sub8 > i - 8 * k, ge, gt)
            ranks[k] = ranks[k] + inc
    rank = jnp.concatenate(ranks + [jnp.zeros((LANES - 8 * n_grp, t), F32)], axis=0)
    sel = (rank < n_sel) & (jrow <= cur)
    selb = jnp.where(sel | (jrow >= n_slc), 0.0, NEG).astype(BF16)
    sel_ref[...] = lax.dot_general(eye_ref[...], selb, _NT, preferred_element_type=F32).astype(BF16)


def _cmp_topk(q_arr, kvc, gates, *, batch, seq, hs, t):
    G = N_KV_NSA
    nc = kvc.shape[2]
    n_slc = seq // SLC_BLOCK
    assert n_slc <= LANES and nc == seq // CMP_STRIDE
    qw = hs * LANES
    start = np.arange(nc) * CMP_STRIDE
    js = np.arange(LANES) * SLC_BLOCK
    ov = ((start[None, :] < js[:, None] + SLC_BLOCK) & (start[None, :] + CMP_BLOCK > js[:, None]))
    ov = ov & (np.arange(nc)[None, :] < nc - 1) & (np.arange(LANES)[:, None] < n_slc)
    kern = functools.partial(_cmp_topk_kernel, hs=hs, t=t, nc=nc, n_slc=n_slc,
                             n_sel=min(N_SELECT, n_slc))
    return pl.pallas_call(
        kern,
        grid=(batch, G, seq // t),
        in_specs=[
            pl.BlockSpec((None, t, qw), lambda b, g, i: (b, i, g)),
            pl.BlockSpec((None, None, nc, HEAD_DIM), lambda b, g, i: (0, b * G + g, 0, 0)),
            pl.BlockSpec((None, None, nc, HEAD_DIM), lambda b, g, i: (1, b * G + g, 0, 0)),
            pl.BlockSpec((None, t, LANES), lambda b, g, i: (b, i, 0)),
            pl.BlockSpec((LANES, nc), lambda b, g, i: (0, 0)),
            pl.BlockSpec((t, t), lambda b, g, i: (0, 0)),
        ],
        out_specs=[
            pl.BlockSpec((None, t, qw), lambda b, g, i: (b, i, g)),
            pl.BlockSpec((None, None, t, LANES), lambda b, g, i: (b, g, i, 0)),
        ],
        out_shape=[jax.ShapeDtypeStruct((batch, seq, G * qw), F32),
                   jax.ShapeDtypeStruct((batch, G, seq, LANES), BF16)],
        scratch_shapes=[pltpu.VMEM((LANES, t), F32)],
        compiler_params=_params("parallel", "parallel", "arbitrary"),
        name="nsa_cmp_topk",
    )(q_arr, kvc, kvc, gates, jnp.asarray(ov, BF16), jnp.eye(t, dtype=BF16))


def _rope_perm():
    half = ROPE_DIM // 2
    perm = np.arange(LANES)
    perm[half:ROPE_DIM] = np.arange(LANES // 2, LANES // 2 + half)
    perm[LANES // 2:LANES // 2 + half] = np.arange(half, ROPE_DIM)
    return perm


def _permute_heads(w):
    return w.reshape(w.shape[:-1] + (-1, LANES))[..., _rope_perm()].reshape(w.shape)


def _rope_tabs(pos):
    half = ROPE_DIM // 2
    inv = 1.0 / (ROPE_THETA ** (jnp.arange(0, ROPE_DIM, 2, dtype=F32) / ROPE_DIM))
    ang = pos.astype(F32)[:, None] * inv
    cos, sin = jnp.cos(ang), jnp.sin(ang)
    n = pos.shape[0]
    second = slice(LANES // 2, LANES // 2 + half)
    c = jnp.ones((n, LANES), F32).at[:, :half].set(cos).at[:, second].set(cos)
    s = jnp.zeros((n, LANES), F32).at[:, :half].set(-sin).at[:, second].set(sin)
    return jnp.stack([c, s])


def _pad_rows(g, rows=8):
    return jnp.zeros((rows, g.shape[-1]), F32).at[:g.shape[0]].set(g)


def _pad_cols(w, n):
    return jnp.zeros((w.shape[0], n), w.dtype).at[:, :w.shape[1]].set(w)


def kernel(x, ln_mix_g, ln_mlp_g, w_mlp_up, w_mlp_down, even_w_in, even_b_f, even_w_out,
           even_g_q_fox, even_g_k_fox, even_g_q_dil, even_g_k_dil, odd_w_in, odd_w_out,
           odd_phi_k_pe, odd_phi_k_w1, odd_phi_k_w2, odd_phi_v_pe, odd_phi_v_w1, odd_phi_v_w2,
           odd_g_q, odd_g_kc, odd_g_ks, odd_g_kw):
    B, S, D = x.shape
    T = B * S
    n_heads = D // HEAD_DIM
    hf = n_heads // 2
    hd = n_heads - hf
    fw, dw = hf * HEAD_DIM, hd * HEAD_DIM
    G = N_KV_NSA
    hpg = n_heads // G
    qw, kvw = n_heads * HEAD_DIM, G * HEAD_DIM
    depth = ln_mix_g.shape[0]

    tm = 512
    tm_proj = 1024
    t_fox = 512
    t_band = 256
    tk_sel = 512
    t_dil = 512
    t_cmp = 256
    tabs = _rope_tabs(jnp.arange(S))
    nc = S // CMP_STRIDE
    tabs_c = _rope_tabs(jnp.arange(nc) * CMP_STRIDE + CMP_BLOCK - 1)

    causal_w = lambda d: (d >= 0).astype(np.float64)
    win_w = lambda d: ((d >= 0) & (d < WINDOW_NSA)).astype(np.float64)
    dil_span = max(w for w, _ in DILATED_PATTERNS)
    fox_bias = _bias_table(t_fox, t_fox, 1, causal_w)
    dil_bias = _bias_table(t_dil, t_dil, (dil_span + t_dil - 1) // t_dil + 1, _dilated_weight)
    sel_bias = _bias_table(tk_sel, tk_sel, 1, causal_w)
    win_back = -(-(WINDOW_NSA - 1) // t_band) * t_band
    win_bias = _bias_table(t_band, win_back + t_band, win_back // t_band + 1, win_w)
    onehot_blk = jnp.asarray(
        (np.arange(S)[:, None] // SLC_BLOCK == np.arange(LANES)[None, :]).astype(np.float32), BF16)

    h = x.reshape(T, D)
    for layer in range(depth):
        i = layer // 2
        if layer % 2 == 0:
            w_in = even_w_in[i]
            c_qd = 4 * fw + hf
            w_main = jnp.concatenate(
                [w_in[:, :4 * fw], _permute_heads(w_in[:, c_qd:c_qd + 2 * dw]),
                 w_in[:, c_qd + 2 * dw:]], axis=1).astype(BF16)
            w_f = _pad_cols(w_in[:, 4 * fw:4 * fw + hf], LANES).astype(BF16)
            gains = _pad_rows(jnp.stack([even_g_q_fox[i], even_g_k_fox[i],
                                         _permute_heads(even_g_q_dil[i]),
                                         _permute_heads(even_g_k_dil[i])]))
            segs = [Seg(fw, norm=True, gain=0, scale=QSCALE), Seg(fw, norm=True, gain=1), Seg(fw),
                    Seg(fw, sigmoid=True),
                    Seg(dw, norm=True, gain=2, rope=True, scale=QSCALE),
                    Seg(dw, norm=True, gain=3, rope=True), Seg(dw)]
            proj, fproj = _norm_proj(h, ln_mix_g[layer], w_main, w_f, gains, tabs, segs, seq=S,
                                     tm=tm_proj, tn=min(1024, fw, dw), gate_sigmoid=False)
            proj = proj.reshape(B, S, -1)
            qaug, kaug = _fox_aug(fproj, even_b_f[i], batch=B, seq=S, hf=hf, ts=512)
            o_a = _flash(proj, 0, proj, fw, proj, 2 * fw, fox_bias, n_kv=hf, hs=1, tq=t_fox,
                         tk=t_fox, mode="causal", nk=min(4, hf), out_dtype=BF16,
                         qaug=qaug.reshape(B, S, -1),
                         kaug=kaug.reshape(B, S, -1), gate=proj, gate_mode="elem",
                         gate_col0=3 * fw)
            o_b = _flash(proj, 4 * fw, proj, 4 * fw + dw, proj, 4 * fw + 2 * dw, dil_bias, n_kv=hd,
                         hs=1, tq=t_dil, tk=t_dil, mode="band", nk=min(4, hd), out_dtype=BF16)
            w_out = even_w_out[i].astype(BF16)
            h = _out_proj([o_a.reshape(T, fw), o_b.reshape(T, dw)], [w_out[:fw], w_out[fw:]], h,
                          tm=tm_proj, tn=1024)
        else:
            w_in = odd_w_in[i]
            n_main = qw + 6 * kvw
            c_kc, c_vc, c_ks, c_kw, c_vs, c_vw = (qw + n * kvw for n in range(6))
            w_main = jnp.concatenate(
                [_permute_heads(w_in[:, :qw]), w_in[:, qw:qw + 2 * kvw],
                 _permute_heads(w_in[:, qw + 2 * kvw:qw + 3 * kvw]),
                 _permute_heads(w_in[:, qw + 4 * kvw:qw + 5 * kvw]),
                 w_in[:, qw + 3 * kvw:qw + 4 * kvw], w_in[:, qw + 5 * kvw:n_main]],
                axis=1).astype(BF16)
            w_g = _pad_cols(w_in[:, n_main:], LANES).astype(BF16)
            gains = _pad_rows(_permute_heads(jnp.stack([odd_g_q[i], odd_g_ks[i], odd_g_kw[i]])))
            segs = [Seg(qw, norm=True, gain=0, rope=True, scale=QSCALE), Seg(2 * kvw, chunks=True),
                    Seg(2 * kvw, norm=True, gain=1, n_gains=2, rope=True), Seg(2 * kvw)]
            proj, gates, kvraw = _norm_proj(h, ln_mix_g[layer], w_main, w_g, gains, tabs, segs,
                                            seq=S, tm=tm_proj, tn=min(1024, 2 * kvw),
                                            gate_sigmoid=True)
            proj, gates = proj.reshape(B, S, -1), gates.reshape(B, S, LANES)
            x2 = kvraw.reshape(2, B * G, nc, CMP_STRIDE * HEAD_DIM)
            pe = jnp.stack([odd_phi_k_pe[i], odd_phi_v_pe[i]]).reshape(2, 1, CMP_BLOCK * HEAD_DIM)
            w1 = jnp.stack([odd_phi_k_w1[i], odd_phi_v_w1[i]]).astype(BF16)
            w2 = jnp.stack([_permute_heads(odd_phi_k_w2[i]), odd_phi_v_w2[i]]).astype(BF16)
            kvc = _compress(x2, pe, w1, w2, _permute_heads(odd_g_kc[i]), tabs_c)
            o_c, selb = _cmp_topk(proj, kvc, gates, batch=B, seq=S, hs=hpg, t=t_cmp)
            o_cs = _flash(proj, 0, proj, c_ks, proj, c_vs, sel_bias, n_kv=G,
                          hs=hpg, tq=tk_sel, tk=tk_sel, mode="causal", split=hpg, out_dtype=F32,
                          qaug=selb,
                          kaug=onehot_blk, gate=gates, gate_mode="col", gate_base=1, addin=o_c)
            o = _flash(proj, 0, proj, c_kw, proj, c_vw, win_bias, n_kv=G, hs=hpg,
                       tq=t_band, tk=win_back + t_band, mode="window", split=hpg, out_dtype=BF16,
                       gate=gates,
                       gate_mode="col", gate_base=2, addin=o_cs)
            h = _out_proj([o.reshape(T, qw)], [odd_w_out[i].astype(BF16)], h, tm=tm_proj, tn=1024)
        h = _mlp(h, ln_mlp_g[layer], w_mlp_up[layer].astype(BF16), w_mlp_down[layer].astype(BF16),
                 tm=tm, tf=1024)
    return h.reshape(B, S, D)
```

```python
import functools
from typing import NamedTuple

import numpy as np
import jax
import jax.numpy as jnp
from jax import lax
from jax.experimental import pallas as pl
from jax.experimental.pallas import tpu as pltpu

HEAD_DIM = 128
LANES = 128
N_KV_NSA = 4
DILATED_PATTERNS = ((128, 1), (512, 4), (2048, 16))
CMP_BLOCK = 32
CMP_STRIDE = 16
SLC_BLOCK = 64
N_SELECT = 16
WINDOW_NSA = 512
ROPE_THETA = 500000.0
ROPE_DIM = HEAD_DIM // 4
EPS = 1e-6
NEG = -1e30
FORCED_SCORE = 1e9
SCALE = HEAD_DIM ** -0.5
LOG2E = float(np.log2(np.e))
QSCALE = SCALE * LOG2E
VMEM_LIMIT = 56 * 1024 * 1024

BF16 = jnp.bfloat16
F32 = jnp.float32
_NT = (((1,), (1,)), ((), ()))


def _params(*sem):
    return pltpu.CompilerParams(dimension_semantics=sem, vmem_limit_bytes=VMEM_LIMIT)


def _split3(x):
    hi = x.astype(BF16)
    r1 = x - hi.astype(F32)
    mid = r1.astype(BF16)
    lo = (r1 - mid.astype(F32)).astype(BF16)
    return hi, mid, lo


class Seg(NamedTuple):
    width: int
    norm: bool = False
    gain: int = 0
    n_gains: int = 1
    rope: bool = False
    scale: float = 1.0
    sigmoid: bool = False
    chunks: bool = False


def _rope(y, tab_ref):
    return y * tab_ref[0] + pltpu.roll(y, LANES // 2, 1) * tab_ref[1]


def _proj_kernel(*refs, tiles, tn, gate_sigmoid):
    x_ref, g_ref, w_ref, wg_ref, gains_ref, tab_ref, o_ref, og_ref = refs[:8]
    oc_ref = refs[8] if any(t.chunks for t in tiles) else None
    xn_ref, acc0, acc1, r_scr = refs[-4:]
    accs = (acc0, acc1)
    j = pl.program_id(1)
    sub = min(tn, 2 * LANES)

    def matmul(t):
        for c in range(tn // sub):
            accs[t % 2][:, c * sub:(c + 1) * sub] = jnp.dot(
                xn_ref[...], w_ref[:, c * sub:(c + 1) * sub], preferred_element_type=F32)

    def epilogue(t):
        seg, acc = tiles[t], accs[t % 2]
        heads = [slice(hd * LANES, (hd + 1) * LANES) for hd in range(tn // LANES)]
        if not seg.norm:
            y = acc[...]
            y = 1.0 / (1.0 + jnp.exp(-y)) if seg.sigmoid else y
            o_ref[...] = y.astype(o_ref.dtype)
            if seg.chunks:
                for hd, sl in enumerate(heads):
                    oc_ref[hd // N_KV_NSA, hd % N_KV_NSA] = y[:, sl].astype(oc_ref.dtype)
            return
        for sl in heads:
            y = acc[:, sl]
            r = lax.rsqrt(jnp.mean(y * y, axis=-1, keepdims=True) + EPS)
            r_scr[:, sl] = jnp.broadcast_to(r, y.shape)
        for hd, sl in enumerate(heads):
            gi = seg.gain + hd // (tn // LANES // seg.n_gains)
            acc[:, sl] = acc[:, sl] * r_scr[:, sl] * gains_ref[gi:gi + 1, :]
        for sl in heads:
            y = _rope(acc[:, sl], tab_ref) if seg.rope else acc[:, sl]
            if seg.scale != 1.0:
                y = y * seg.scale
            o_ref[:, sl] = y.astype(o_ref.dtype)

    @pl.when(j == 0)
    def _():
        x = x_ref[...]
        ms = jnp.mean(x * x, axis=-1, keepdims=True)
        xn_ref[...] = (x * lax.rsqrt(ms + EPS) * g_ref[...]).astype(BF16)
        gate = jnp.dot(xn_ref[...], wg_ref[...], preferred_element_type=F32)
        og_ref[...] = 1.0 / (1.0 + jnp.exp(-gate)) if gate_sigmoid else gate
        matmul(0)

    for t in range(len(tiles)):
        @pl.when(j == t + 1)
        def _(t=t):
            epilogue(t)
            if t + 1 < len(tiles):
                matmul(t + 1)


def _norm_proj(x, g, w, w_gate, gains, tabs, segs, *, seq, tm, tn, gate_sigmoid):
    T, D = x.shape
    N = w.shape[1]
    assert sum(s.width for s in segs) == N and all(s.width % tn == 0 for s in segs)
    assert T % tm == 0 and seq % tm == 0 and tn % LANES == 0
    assert all(s.norm or (not s.rope and s.scale == 1.0) for s in segs)
    assert not any(s.norm and (s.sigmoid or s.chunks) for s in segs)
    assert all(s.width == tn == 2 * N_KV_NSA * LANES for s in segs if s.chunks)
    nseq = seq // tm
    nj = N // tn
    tiles = []
    for s in segs:
        nt = s.width // tn
        assert s.n_gains == 1 or s.n_gains % nt == 0
        for k in range(nt):
            tiles.append(s._replace(width=tn, n_gains=max(s.n_gains // nt, 1),
                                    gain=s.gain + k * (s.n_gains // nt)))
    out_specs = [pl.BlockSpec((tm, tn), lambda i, j: (i, jnp.maximum(j - 1, 0))),
                 pl.BlockSpec((tm, LANES), lambda i, j: (i, 0))]
    out_shape = [jax.ShapeDtypeStruct((T, N), BF16), jax.ShapeDtypeStruct((T, LANES), F32)]
    if any(s.chunks for s in segs):
        out_specs.append(pl.BlockSpec((2, None, N_KV_NSA, tm, LANES),
                                      lambda i, j: (0, i // nseq, 0, i % nseq, 0)))
        out_shape.append(jax.ShapeDtypeStruct((2, T // seq, N_KV_NSA, seq, LANES), BF16))
    return pl.pallas_call(
        functools.partial(_proj_kernel, tiles=tuple(tiles), tn=tn, gate_sigmoid=gate_sigmoid),
        grid=(T // tm, nj + 1),
        in_specs=[
            pl.BlockSpec((tm, D), lambda i, j: (i, 0)),
            pl.BlockSpec((1, D), lambda i, j: (0, 0)),
            pl.BlockSpec((D, tn), lambda i, j: (0, jnp.minimum(j, nj - 1))),
            pl.BlockSpec((D, LANES), lambda i, j: (0, 0)),
            pl.BlockSpec(gains.shape, lambda i, j: (0, 0)),
            pl.BlockSpec((2, tm, LANES), lambda i, j: (0, i % nseq, 0)),
        ],
        out_specs=out_specs,
        out_shape=out_shape,
        scratch_shapes=[pltpu.VMEM((tm, D), BF16), pltpu.VMEM((tm, tn), F32),
                        pltpu.VMEM((tm, tn), F32), pltpu.VMEM((tm, tn), F32)],
        compiler_params=_params("parallel", "arbitrary"),
        name="norm_proj",
    )(x, g.reshape(1, D), w, w_gate, gains, tabs)


def _outproj_kernel(*refs, n_parts):
    a_refs = refs[:n_parts]
    w_refs = refs[n_parts:2 * n_parts]
    h_ref = refs[2 * n_parts]
    o_ref = refs[2 * n_parts + 1]
    acc = h_ref[...]
    for a_ref, w_ref in zip(a_refs, w_refs):
        acc = acc + jnp.dot(a_ref[...], w_ref[...], preferred_element_type=F32)
    o_ref[...] = acc


def _out_proj(parts, weights, h, *, tm, tn):
    T, D = h.shape
    n = len(parts)
    in_specs = [pl.BlockSpec((tm, p.shape[1]), lambda i, j: (i, 0)) for p in parts]
    in_specs += [pl.BlockSpec((w.shape[0], tn), lambda i, j: (0, j)) for w in weights]
    in_specs += [pl.BlockSpec((tm, tn), lambda i, j: (i, j))]
    return pl.pallas_call(
        functools.partial(_outproj_kernel, n_parts=n),
        grid=(T // tm, D // tn),
        in_specs=in_specs,
        out_specs=pl.BlockSpec((tm, tn), lambda i, j: (i, j)),
        out_shape=jax.ShapeDtypeStruct((T, D), F32),
        compiler_params=_params("parallel", "arbitrary"),
        name="out_proj",
    )(*parts, *weights, h)


def _mlp_kernel(h_ref, g_ref, wu_ref, wd_ref, o_ref, xn_ref):
    f = pl.program_id(1)

    @pl.when(f == 0)
    def _():
        x = h_ref[...]
        ms = jnp.mean(x * x, axis=-1, keepdims=True)
        xn_ref[...] = (x * lax.rsqrt(ms + EPS) * g_ref[...]).astype(BF16)
        o_ref[...] = x

    tf = wu_ref.shape[1]
    sub = min(tf, 4 * LANES)
    total = None
    for c in range(tf // sub):
        u = jnp.dot(xn_ref[...], wu_ref[:, c * sub:(c + 1) * sub], preferred_element_type=F32)
        a = jnp.square(jnp.maximum(u, 0.0)).astype(BF16)
        part = jnp.dot(a, wd_ref[c * sub:(c + 1) * sub, :], preferred_element_type=F32)
        total = part if total is None else total + part
    o_ref[...] += total


def _mlp(h, g, w_up, w_down, *, tm, tf):
    T, D = h.shape
    FF = w_up.shape[1]
    return pl.pallas_call(
        _mlp_kernel,
        grid=(T // tm, FF // tf),
        in_specs=[
            pl.BlockSpec((tm, D), lambda i, f: (i, 0)),
            pl.BlockSpec((1, D), lambda i, f: (0, 0)),
            pl.BlockSpec((D, tf), lambda i, f: (0, f)),
            pl.BlockSpec((tf, D), lambda i, f: (f, 0)),
        ],
        out_specs=pl.BlockSpec((tm, D), lambda i, f: (i, 0)),
        out_shape=jax.ShapeDtypeStruct((T, D), F32),
        scratch_shapes=[pltpu.VMEM((tm, D), BF16)],
        compiler_params=_params("parallel", "arbitrary"),
        name="sq_relu_mlp",
    )(h, g.reshape(1, D), w_up, w_down)


def _foxaug_kernel(f_ref, bf_ref, tri_ref, pq_ref, pk_ref, oq_ref, ok_ref, carry_ref, *, ts):
    @pl.when(pl.program_id(1) == 0)
    def _():
        carry_ref[...] = jnp.zeros_like(carry_ref)

    x = f_ref[...] + bf_ref[...]
    logf = jnp.minimum(x, 0.0) - jnp.log(1.0 + jnp.exp(-jnp.abs(x)))
    tri = tri_ref[...]
    c = carry_ref[...]
    for part in _split3(logf):
        c = c + jnp.dot(tri, part, preferred_element_type=F32)
    carry_ref[...] = c[ts - 1:ts, :]
    parts = jnp.concatenate(list(_split3(c * LOG2E)) + [jnp.ones((ts, LANES), BF16)], axis=1)
    oq_ref[...] = jnp.dot(parts, pq_ref[...], preferred_element_type=F32).astype(BF16)
    ok_ref[...] = jnp.dot(parts, pk_ref[...], preferred_element_type=F32).astype(BF16)


def _fox_aug(fproj, b_f, *, batch, seq, hf, ts):
    T = fproj.shape[0]
    ns = seq // ts
    tri = np.tril(np.ones((ts, ts), np.float32))
    pq = np.zeros((4 * LANES, hf * LANES), np.float32)
    pk = np.zeros((4 * LANES, hf * LANES), np.float32)
    for h in range(hf):
        for p in range(3):
            pq[p * LANES + h, h * LANES + p] = 1.0
            pq[3 * LANES, h * LANES + 3 + p] = 1.0
            pk[3 * LANES, h * LANES + p] = 1.0
            pk[p * LANES + h, h * LANES + 3 + p] = -1.0
    bf = jnp.zeros((1, LANES), F32).at[0, :hf].set(b_f)
    out = jax.ShapeDtypeStruct((T, hf * LANES), BF16)
    return pl.pallas_call(
        functools.partial(_foxaug_kernel, ts=ts),
        grid=(batch, ns),
        in_specs=[
            pl.BlockSpec((ts, LANES), lambda b, s: (b * ns + s, 0)),
            pl.BlockSpec((1, LANES), lambda b, s: (0, 0)),
            pl.BlockSpec((ts, ts), lambda b, s: (0, 0)),
            pl.BlockSpec(pq.shape, lambda b, s: (0, 0)),
            pl.BlockSpec(pk.shape, lambda b, s: (0, 0)),
        ],
        out_specs=[pl.BlockSpec((ts, hf * LANES), lambda b, s: (b * ns + s, 0))] * 2,
        out_shape=[out, out],
        scratch_shapes=[pltpu.VMEM((1, LANES), F32)],
        compiler_params=_params("parallel", "arbitrary"),
        name="fox_aug",
    )(fproj, bf, jnp.asarray(tri, BF16), jnp.asarray(pq, BF16), jnp.asarray(pk, BF16))


def _flash_kernel(*refs, nk, split, hs, tq, tk, mode, has_aug, gate_mode, gate_base, has_addin):
    it = iter(refs)
    q_ref = next(it)
    qa_ref = next(it) if has_aug else None
    k_ref = next(it)
    ka_ref = next(it) if has_aug else None
    v_ref = next(it)
    bias_ref = next(it)
    gate_ref = next(it) if gate_mode else None
    add_ref = next(it) if has_addin else None
    o_ref = next(it)
    scr = list(it)
    per = 1 if mode == "window" else 4
    nch = nk * split
    hc = hs // split
    chains = [scr[c * per:(c + 1) * per] for c in range(nch)]

    qi = pl.program_id(2)
    M = hc * tq
    ones = jnp.ones((tk, LANES), BF16)
    lanes = lambda c: slice((c // split) * LANES, (c // split + 1) * LANES)
    head0 = lambda c: (c // split) * hs + (c % split) * hc

    for c in range(nch):
        q_scr = chains[c][0]
        for h in range(hc):
            col = (head0(c) + h) * LANES
            q_scr[h * tq:(h + 1) * tq, :LANES] = q_ref[:, col:col + LANES]
            if has_aug:
                q_scr[h * tq:(h + 1) * tq, LANES:] = qa_ref[:, lanes(c)]

    def scores(c, k0):
        k = k_ref[pl.ds(k0, tk), lanes(c)]
        if has_aug:
            k = jnp.concatenate([k, ka_ref[pl.ds(k0, tk), lanes(c)]], axis=1)
        return lax.dot_general(chains[c][0][...], k, _NT, preferred_element_type=F32)

    def biased(s, off):
        b = bias_ref[off]
        return (s.reshape(hc, tq, tk) + b[None]).reshape(M, tk) if hc > 1 else s + b

    def values(c, k0):
        return jnp.concatenate([v_ref[pl.ds(k0, tk), lanes(c)], ones], axis=1)

    accs = []
    if mode == "window":
        back = tk - tq
        k0 = pl.multiple_of(jnp.maximum(qi * tq - back, 0), tq)
        for c in range(nch):
            s = biased(scores(c, k0), jnp.minimum(qi, back // tq))
            p = jnp.exp2(s - jnp.max(s, axis=1, keepdims=True))
            accs.append(jnp.dot(p.astype(BF16), values(c, k0), preferred_element_type=F32))
    else:
        for c in range(nch):
            _, m_scr, acc_scr, _ = chains[c]
            m_scr[...] = jnp.full(m_scr.shape, NEG, F32)
            acc_scr[...] = jnp.zeros(acc_scr.shape, F32)

        def stage(c, kt):
            chains[c][3][kt & 1] = scores(c, pl.multiple_of(kt * tk, tk))

        def update(c, kt, off):
            _, m_scr, acc_scr, s_scr = chains[c]
            s = s_scr[kt & 1]
            if off is not None:
                s = biased(s, off)
            m_prev = m_scr[...]
            m_new = jnp.maximum(m_prev, jnp.max(s, axis=1, keepdims=True))
            alpha = jnp.exp2(m_prev - m_new)
            p = jnp.exp2(s - jnp.concatenate([m_new] * (tk // LANES), axis=1))
            pv = jnp.dot(p.astype(BF16), values(c, pl.multiple_of(kt * tk, tk)),
                         preferred_element_type=F32)
            acc_scr[...] = jnp.concatenate([alpha, alpha], axis=1) * acc_scr[...] + pv
            m_scr[...] = m_new

        if mode == "causal":
            last = (qi * tq) // tk
            first, last_off = 0, qi - last * (tk // tq)
        else:
            last = qi
            first, last_off = jnp.maximum(qi - (bias_ref.shape[0] - 1), 0), 0

        for c in range(nch):
            stage(c, first)

        def body(kt, carry):
            for c in range(nch):
                update(c, kt, None if mode == "causal" else qi - kt)
                stage(c, kt + 1)
            return carry

        lax.fori_loop(first, last, body, 0)
        for c in range(nch):
            update(c, last, last_off)
            accs.append(chains[c][2][...])

    for c in range(nch):
        o = accs[c][:, :LANES] * (1.0 / accs[c][:, LANES:])
        for h in range(hc):
            oh = o[h * tq:(h + 1) * tq, :]
            head = head0(c) + h
            sl = slice(head * LANES, (head + 1) * LANES)
            if gate_mode == "elem":
                oh = oh * gate_ref[:, sl].astype(F32)
            elif gate_mode == "col":
                idx = (pl.program_id(1) * hs + head) * 3 + gate_base
                lane = lax.broadcasted_iota(jnp.int32, (tq, LANES), 1)
                oh = oh * jnp.sum(jnp.where(lane == idx, gate_ref[...], 0.0), axis=1, keepdims=True)
            if has_addin:
                oh = oh + add_ref[:, sl]
            o_ref[:, sl] = oh.astype(o_ref.dtype)


def _flash(q_arr, q_col0, k_arr, k_col0, v_arr, v_col0, bias, *, n_kv, hs, tq, tk, mode, out_dtype,
           nk=1, split=1, qaug=None, kaug=None, gate=None, gate_mode=None, gate_col0=0, gate_base=0,
           addin=None):
    B, S, _ = q_arr.shape
    qw = hs * LANES
    assert q_col0 % (nk * qw) == 0 and k_col0 % (nk * LANES) == 0 and v_col0 % (nk * LANES) == 0
    assert S % tq == 0 and S % tk == 0 if mode != "window" else (S % tq == 0 and tk <= S)
    assert tk % tq == 0 and (mode != "band" or tq == tk) and bias.shape[1:] == (tq, tk)
    assert n_kv % nk == 0 and hs % split == 0
    assert nk == 1 or (gate_mode != "col" and (qaug is None or qaug.ndim == 3))
    qb, kb, vb = q_col0 // (nk * qw), k_col0 // (nk * LANES), v_col0 // (nk * LANES)
    assert (qaug is None) == (kaug is None)
    dk = LANES if qaug is None else 2 * LANES
    ins, specs = [q_arr], [pl.BlockSpec((None, tq, nk * qw), lambda b, h, i: (b, i, qb + h))]
    if qaug is not None:
        ins.append(qaug)
        if qaug.ndim == 3:
            specs.append(pl.BlockSpec((None, tq, nk * LANES), lambda b, h, i: (b, i, h)))
        else:
            specs.append(pl.BlockSpec((None, None, tq, LANES), lambda b, h, i: (b, h, i, 0)))
    ins.append(k_arr)
    specs.append(pl.BlockSpec((None, S, nk * LANES), lambda b, h, i: (b, 0, kb + h)))
    if kaug is not None:
        ins.append(kaug)
        if kaug.ndim == 3:
            specs.append(pl.BlockSpec((None, S, nk * LANES), lambda b, h, i: (b, 0, h)))
        else:
            specs.append(pl.BlockSpec((S, LANES), lambda b, h, i: (0, 0)))
    ins.append(v_arr)
    specs.append(pl.BlockSpec((None, S, nk * LANES), lambda b, h, i: (b, 0, vb + h)))
    ins.append(bias)
    specs.append(pl.BlockSpec(bias.shape, lambda b, h, i: (0, 0, 0)))
    if gate_mode == "elem":
        assert gate_col0 % (nk * LANES) == 0
        gb = gate_col0 // (nk * LANES)
        ins.append(gate)
        specs.append(pl.BlockSpec((None, tq, nk * LANES), lambda b, h, i: (b, i, gb + h)))
    elif gate_mode == "col":
        ins.append(gate)
        specs.append(pl.BlockSpec((None, tq, LANES), lambda b, h, i: (b, i, 0)))
    if addin is not None:
        ins.append(addin)
        specs.append(pl.BlockSpec((None, tq, nk * qw), lambda b, h, i: (b, i, h)))
    kern = functools.partial(
        _flash_kernel, nk=nk, split=split, hs=hs, tq=tq, tk=tk, mode=mode, has_aug=qaug is not None,
        gate_mode=gate_mode, gate_base=gate_base, has_addin=addin is not None)
    mc = hs // split * tq
    scratch = []
    for _ in range(nk * split):
        scratch.append(pltpu.VMEM((mc, dk), BF16))
        if mode != "window":
            scratch += [pltpu.VMEM((mc, LANES), F32), pltpu.VMEM((mc, 2 * LANES), F32),
                        pltpu.VMEM((2, mc, tk), F32)]
    return pl.pallas_call(
        kern,
        grid=(B, n_kv // nk, S // tq),
        in_specs=specs,
        out_specs=pl.BlockSpec((None, tq, nk * qw), lambda b, h, i: (b, i, h)),
        out_shape=jax.ShapeDtypeStruct((B, S, n_kv * qw), out_dtype),
        scratch_shapes=scratch,
        compiler_params=_params("parallel", "parallel", "arbitrary"),
        name="flash_attention",
    )(*ins)


def _bias_table(tq, tk, n, weight_of_distance):
    r = np.arange(tq)[:, None]
    c = np.arange(tk)[None, :]
    tabs = []
    for v in range(n):
        w = weight_of_distance(v * tq + r - c)
        tabs.append(np.where(w > 0, np.log2(np.maximum(w, 1.0)), NEG))
    return jnp.asarray(np.stack(tabs), F32)


def _dilated_weight(d):
    w = np.zeros(d.shape, np.float64)
    for window, dil in DILATED_PATTERNS:
        w += (d >= 0) & (d <= window) & (d % dil == 0)
    return w


def _compress_kernel(x_ref, pe_ref, w1_ref, w2_ref, g_ref, tab_ref, o_ref, *, nc):
    half = CMP_STRIDE * HEAD_DIM
    x = x_ref[...].astype(F32)
    top = (x + pe_ref[:, :half]).astype(BF16)
    bot = (x + pe_ref[:, half:]).astype(BF16)
    a = jnp.dot(top, w1_ref[:half, :], preferred_element_type=F32)
    b = jnp.dot(bot, w1_ref[half:, :], preferred_element_type=F32)
    pre = a + pltpu.roll(b, nc - 1, 0)
    hid = pre * (0.5 * (1.0 + jnp.tanh(np.sqrt(2.0 / np.pi) * (pre + 0.044715 * (pre * pre * pre)))))
    y = jnp.dot(hid.astype(BF16), w2_ref[...], preferred_element_type=F32)

    @pl.when(pl.program_id(0) == 0)
    def _():
        ms = jnp.mean(y * y, axis=-1, keepdims=True)
        o_ref[...] = _rope(y * lax.rsqrt(ms + EPS) * g_ref[...], tab_ref).astype(BF16)

    @pl.when(pl.program_id(0) != 0)
    def _():
        o_ref[...] = y.astype(BF16)


def _compress(x2, pe, w1, w2, g_kc, tabs_c):
    _, BG, nc, width = x2.shape
    return pl.pallas_call(
        functools.partial(_compress_kernel, nc=nc),
        grid=(2, BG),
        in_specs=[
            pl.BlockSpec((None, None, nc, width), lambda s, i: (s, i, 0, 0)),
            pl.BlockSpec((None, 1, 2 * width), lambda s, i: (s, 0, 0)),
            pl.BlockSpec((None, 2 * width, HEAD_DIM), lambda s, i: (s, 0, 0)),
            pl.BlockSpec((None, HEAD_DIM, HEAD_DIM), lambda s, i: (s, 0, 0)),
            pl.BlockSpec((1, HEAD_DIM), lambda s, i: (0, 0)),
            pl.BlockSpec((2, nc, LANES), lambda s, i: (0, 0, 0)),
        ],
        out_specs=pl.BlockSpec((None, None, nc, HEAD_DIM), lambda s, i: (s, i, 0, 0)),
        out_shape=jax.ShapeDtypeStruct((2, BG, nc, HEAD_DIM), BF16),
        compiler_params=_params("arbitrary", "arbitrary"),
        name="nsa_compress",
    )(x2, pe, w1, w2, g_kc.reshape(1, HEAD_DIM), tabs_c)


def _cmp_topk_kernel(q_ref, kc_ref, vc_ref, gate_ref, ov_ref, eye_ref, oc_ref, sel_ref, v_scr, *,
                     hs, t, nc, n_slc, n_sel):
    g = pl.program_id(1)
    q0 = pl.program_id(2) * t
    M = hs * t
    q = jnp.concatenate([q_ref[:, h * LANES:(h + 1) * LANES] for h in range(hs)], axis=0)
    s = lax.dot_general(q, kc_ref[...], _NT, preferred_element_type=F32)
    qpos = q0 + (lax.broadcasted_iota(jnp.int32, (M, nc), 0) & (t - 1))
    n = lax.broadcasted_iota(jnp.int32, (M, nc), 1)
    mask = n * CMP_STRIDE + (CMP_BLOCK - 1) <= qpos
    s = jnp.where(mask, s, NEG)
    m = jnp.max(s, axis=1, keepdims=True)
    e = jnp.where(mask, jnp.exp2(s - m), 0.0)
    p = e / jnp.maximum(jnp.sum(e, axis=1, keepdims=True), 1e-30)
    oc = jnp.dot(p.astype(BF16), vc_ref[...], preferred_element_type=F32)
    lane = lax.broadcasted_iota(jnp.int32, (t, LANES), 1)
    gates = gate_ref[...]
    psum = jnp.zeros((t, nc), F32)
    for h in range(hs):
        idx = (g * hs + h) * 3
        gcol = jnp.sum(jnp.where(lane == idx, gates, 0.0), axis=1, keepdims=True)
        oc_ref[:, h * LANES:(h + 1) * LANES] = oc[h * t:(h + 1) * t, :] * gcol
        psum = psum + p[h * t:(h + 1) * t, :]

    imp = jnp.zeros((LANES, t), F32)
    for part in _split3(psum):
        imp = imp + lax.dot_general(ov_ref[...], part, _NT, preferred_element_type=F32)
    jrow = lax.broadcasted_iota(jnp.int32, (LANES, t), 0)
    cur = (q0 + lax.broadcasted_iota(jnp.int32, (LANES, t), 1)) // SLC_BLOCK
    forced = (jrow == 0) | (jrow == cur) | (jrow == cur - 1)
    val = jnp.where(forced, FORCED_SCORE, imp)
    val = jnp.where(jrow <= cur, val, NEG)
    v_scr[...] = val

    n_grp = -(-n_slc // 8)
    sub8 = lax.broadcasted_iota(jnp.int32, (8, t), 0)
    vals = [val[8 * k:8 * k + 8, :] for k in range(n_grp)]
    ranks = [jnp.zeros((8, t), F32) for _ in range(n_grp)]
    for i in range(n_slc):
        vi = jnp.broadcast_to(v_scr[i:i + 1, :], (8, t))
        for k in range(n_grp):
            ge = jnp.where(vi >= vals[k], 1.0, 0.0)
            gt = jnp.where(vi > vals[k], 1.0, 0.0)
            if 8 * k > i:
                inc = ge
            elif 8 * k + 7 < i:
                inc = gt
            else:
                inc = jnp.where(sub8 > i - 8 * k, ge, gt)
            ranks[k] = ranks[k] + inc
    rank = jnp.concatenate(ranks + [jnp.zeros((LANES - 8 * n_grp, t), F32)], axis=0)
    sel = (rank < n_sel) & (jrow <= cur)
    selb = jnp.where(sel | (jrow >= n_slc), 0.0, NEG).astype(BF16)
    sel_ref[...] = lax.dot_general(eye_ref[...], selb, _NT, preferred_element_type=F32).astype(BF16)


def _cmp_topk(q_arr, kvc, gates, *, batch, seq, hs, t):
    G = N_KV_NSA
    nc = kvc.shape[2]
    n_slc = seq // SLC_BLOCK
    assert n_slc <= LANES and nc == seq // CMP_STRIDE
    qw = hs * LANES
    start = np.arange(nc) * CMP_STRIDE
    js = np.arange(LANES) * SLC_BLOCK
    ov = ((start[None, :] < js[:, None] + SLC_BLOCK) & (start[None, :] + CMP_BLOCK > js[:, None]))
    ov = ov & (np.arange(nc)[None, :] < nc - 1) & (np.arange(LANES)[:, None] < n_slc)
    kern = functools.partial(_cmp_topk_kernel, hs=hs, t=t, nc=nc, n_slc=n_slc,
                             n_sel=min(N_SELECT, n_slc))
    return pl.pallas_call(
        kern,
        grid=(batch, G, seq // t),
        in_specs=[
            pl.BlockSpec((None, t, qw), lambda b, g, i: (b, i, g)),
            pl.BlockSpec((None, None, nc, HEAD_DIM), lambda b, g, i: (0, b * G + g, 0, 0)),
            pl.BlockSpec((None, None, nc, HEAD_DIM), lambda b, g, i: (1, b * G + g, 0, 0)),
            pl.BlockSpec((None, t, LANES), lambda b, g, i: (b, i, 0)),
            pl.BlockSpec((LANES, nc), lambda b, g, i: (0, 0)),
            pl.BlockSpec((t, t), lambda b, g, i: (0, 0)),
        ],
        out_specs=[
            pl.BlockSpec((None, t, qw), lambda b, g, i: (b, i, g)),
            pl.BlockSpec((None, None, t, LANES), lambda b, g, i: (b, g, i, 0)),
        ],
        out_shape=[jax.ShapeDtypeStruct((batch, seq, G * qw), F32),
                   jax.ShapeDtypeStruct((batch, G, seq, LANES), BF16)],
        scratch_shapes=[pltpu.VMEM((LANES, t), F32)],
        compiler_params=_params("parallel", "parallel", "arbitrary"),
        name="nsa_cmp_topk",
    )(q_arr, kvc, kvc, gates, jnp.asarray(ov, BF16), jnp.eye(t, dtype=BF16))


def _rope_perm():
    half = ROPE_DIM // 2
    perm = np.arange(LANES)
    perm[half:ROPE_DIM] = np.arange(LANES // 2, LANES // 2 + half)
    perm[LANES // 2:LANES // 2 + half] = np.arange(half, ROPE_DIM)
    return perm


def _permute_heads(w):
    return w.reshape(w.shape[:-1] + (-1, LANES))[..., _rope_perm()].reshape(w.shape)


def _rope_tabs(pos):
    half = ROPE_DIM // 2
    inv = 1.0 / (ROPE_THETA ** (jnp.arange(0, ROPE_DIM, 2, dtype=F32) / ROPE_DIM))
    ang = pos.astype(F32)[:, None] * inv
    cos, sin = jnp.cos(ang), jnp.sin(ang)
    n = pos.shape[0]
    second = slice(LANES // 2, LANES // 2 + half)
    c = jnp.ones((n, LANES), F32).at[:, :half].set(cos).at[:, second].set(cos)
    s = jnp.zeros((n, LANES), F32).at[:, :half].set(-sin).at[:, second].set(sin)
    return jnp.stack([c, s])


def _pad_rows(g, rows=8):
    return jnp.zeros((rows, g.shape[-1]), F32).at[:g.shape[0]].set(g)


def _pad_cols(w, n):
    return jnp.zeros((w.shape[0], n), w.dtype).at[:, :w.shape[1]].set(w)


def kernel(x, ln_mix_g, ln_mlp_g, w_mlp_up, w_mlp_down, even_w_in, even_b_f, even_w_out,
           even_g_q_fox, even_g_k_fox, even_g_q_dil, even_g_k_dil, odd_w_in, odd_w_out,
           odd_phi_k_pe, odd_phi_k_w1, odd_phi_k_w2, odd_phi_v_pe, odd_phi_v_w1, odd_phi_v_w2,
           odd_g_q, odd_g_kc, odd_g_ks, odd_g_kw):
    B, S, D = x.shape
    T = B * S
    n_heads = D // HEAD_DIM
    hf = n_heads // 2
    hd = n_heads - hf
    fw, dw = hf * HEAD_DIM, hd * HEAD_DIM
    G = N_KV_NSA
    hpg = n_heads // G
    qw, kvw = n_heads * HEAD_DIM, G * HEAD_DIM
    depth = ln_mix_g.shape[0]

    tm = 512
    tm_proj = 1024
    t_fox = 512
    t_band = 256
    tk_sel = 512
    t_dil = 512
    t_cmp = 256
    tabs = _rope_tabs(jnp.arange(S))
    nc = S // CMP_STRIDE
    tabs_c = _rope_tabs(jnp.arange(nc) * CMP_STRIDE + CMP_BLOCK - 1)

    causal_w = lambda d: (d >= 0).astype(np.float64)
    win_w = lambda d: ((d >= 0) & (d < WINDOW_NSA)).astype(np.float64)
    dil_span = max(w for w, _ in DILATED_PATTERNS)
    fox_bias = _bias_table(t_fox, t_fox, 1, causal_w)
    dil_bias = _bias_table(t_dil, t_dil, (dil_span + t_dil - 1) // t_dil + 1, _dilated_weight)
    sel_bias = _bias_table(tk_sel, tk_sel, 1, causal_w)
    win_back = -(-(WINDOW_NSA - 1) // t_band) * t_band
    win_bias = _bias_table(t_band, win_back + t_band, win_back // t_band + 1, win_w)
    onehot_blk = jnp.asarray(
        (np.arange(S)[:, None] // SLC_BLOCK == np.arange(LANES)[None, :]).astype(np.float32), BF16)

    h = x.reshape(T, D)
    for layer in range(depth):
        i = layer // 2
        if layer % 2 == 0:
            w_in = even_w_in[i]
            c_qd = 4 * fw + hf
            w_main = jnp.concatenate(
                [w_in[:, :4 * fw], _permute_heads(w_in[:, c_qd:c_qd + 2 * dw]),
                 w_in[:, c_qd + 2 * dw:]], axis=1).astype(BF16)
            w_f = _pad_cols(w_in[:, 4 * fw:4 * fw + hf], LANES).astype(BF16)
            gains = _pad_rows(jnp.stack([even_g_q_fox[i], even_g_k_fox[i],
                                         _permute_heads(even_g_q_dil[i]),
                                         _permute_heads(even_g_k_dil[i])]))
            segs = [Seg(fw, norm=True, gain=0, scale=QSCALE), Seg(fw, norm=True, gain=1), Seg(fw),
                    Seg(fw, sigmoid=True),
                    Seg(dw, norm=True, gain=2, rope=True, scale=QSCALE),
                    Seg(dw, norm=True, gain=3, rope=True), Seg(dw)]
            proj, fproj = _norm_proj(h, ln_mix_g[layer], w_main, w_f, gains, tabs, segs, seq=S,
                                     tm=tm_proj, tn=min(1024, fw, dw), gate_sigmoid=False)
            proj = proj.reshape(B, S, -1)
            qaug, kaug = _fox_aug(fproj, even_b_f[i], batch=B, seq=S, hf=hf, ts=512)
            o_a = _flash(proj, 0, proj, fw, proj, 2 * fw, fox_bias, n_kv=hf, hs=1, tq=t_fox,
                         tk=t_fox, mode="causal", nk=min(4, hf), out_dtype=BF16,
                         qaug=qaug.reshape(B, S, -1),
                         kaug=kaug.reshape(B, S, -1), gate=proj, gate_mode="elem",
                         gate_col0=3 * fw)
            o_b = _flash(proj, 4 * fw, proj, 4 * fw + dw, proj, 4 * fw + 2 * dw, dil_bias, n_kv=hd,
                         hs=1, tq=t_dil, tk=t_dil, mode="band", nk=min(4, hd), out_dtype=BF16)
            w_out = even_w_out[i].astype(BF16)
            h = _out_proj([o_a.reshape(T, fw), o_b.reshape(T, dw)], [w_out[:fw], w_out[fw:]], h,
                          tm=tm_proj, tn=1024)
        else:
            w_in = odd_w_in[i]
            n_main = qw + 6 * kvw
            c_kc, c_vc, c_ks, c_kw, c_vs, c_vw = (qw + n * kvw for n in range(6))
            w_main = jnp.concatenate(
                [_permute_heads(w_in[:, :qw]), w_in[:, qw:qw + 2 * kvw],
                 _permute_heads(w_in[:, qw + 2 * kvw:qw + 3 * kvw]),
                 _permute_heads(w_in[:, qw + 4 * kvw:qw + 5 * kvw]),
                 w_in[:, qw + 3 * kvw:qw + 4 * kvw], w_in[:, qw + 5 * kvw:n_main]],
                axis=1).astype(BF16)
            w_g = _pad_cols(w_in[:, n_main:], LANES).astype(BF16)
            gains = _pad_rows(_permute_heads(jnp.stack([odd_g_q[i], odd_g_ks[i], odd_g_kw[i]])))
            segs = [Seg(qw, norm=True, gain=0, rope=True, scale=QSCALE), Seg(2 * kvw, chunks=True),
                    Seg(2 * kvw, norm=True, gain=1, n_gains=2, rope=True), Seg(2 * kvw)]
            proj, gates, kvraw = _norm_proj(h, ln_mix_g[layer], w_main, w_g, gains, tabs, segs,
                                            seq=S, tm=tm_proj, tn=min(1024, 2 * kvw),
                                            gate_sigmoid=True)
            proj, gates = proj.reshape(B, S, -1), gates.reshape(B, S, LANES)
            x2 = kvraw.reshape(2, B * G, nc, CMP_STRIDE * HEAD_DIM)
            pe = jnp.stack([odd_phi_k_pe[i], odd_phi_v_pe[i]]).reshape(2, 1, CMP_BLOCK * HEAD_DIM)
            w1 = jnp.stack([odd_phi_k_w1[i], odd_phi_v_w1[i]]).astype(BF16)
            w2 = jnp.stack([_permute_heads(odd_phi_k_w2[i]), odd_phi_v_w2[i]]).astype(BF16)
            kvc = _compress(x2, pe, w1, w2, _permute_heads(odd_g_kc[i]), tabs_c)
            o_c, selb = _cmp_topk(proj, kvc, gates, batch=B, seq=S, hs=hpg, t=t_cmp)
            o_cs = _flash(proj, 0, proj, c_ks, proj, c_vs, sel_bias, n_kv=G,
                          hs=hpg, tq=tk_sel, tk=tk_sel, mode="causal", split=hpg, out_dtype=F32,
                          qaug=selb,
                          kaug=onehot_blk, gate=gates, gate_mode="col", gate_base=1, addin=o_c)
            o = _flash(proj, 0, proj, c_kw, proj, c_vw, win_bias, n_kv=G, hs=hpg,
                       tq=t_band, tk=win_back + t_band, mode="window", split=hpg, out_dtype=BF16,
                       gate=gates,
                       gate_mode="col", gate_base=2, addin=o_cs)
            h = _out_proj([o.reshape(T, qw)], [odd_w_out[i].astype(BF16)], h, tm=tm_proj, tn=1024)
        h = _mlp(h, ln_mlp_g[layer], w_mlp_up[layer].astype(BF16), w_mlp_down[layer].astype(BF16),
                 tm=tm, tf=1024)
    return h.reshape(B, S, D)
```

```python
import functools
from typing import NamedTuple

import numpy as np
import jax
import jax.numpy as jnp
from jax import lax
from jax.experimental import pallas as pl
from jax.experimental.pallas import tpu as pltpu

HEAD_DIM = 128
LANES = 128
N_KV_NSA = 4
DILATED_PATTERNS = ((128, 1), (512, 4), (2048, 16))
CMP_BLOCK = 32
CMP_STRIDE = 16
SLC_BLOCK = 64
N_SELECT = 16
WINDOW_NSA = 512
ROPE_THETA = 500000.0
ROPE_DIM = HEAD_DIM // 4
EPS = 1e-6
NEG = -1e30
FORCED_SCORE = 1e9
SCALE = HEAD_DIM ** -0.5
LOG2E = float(np.log2(np.e))
QSCALE = SCALE * LOG2E
VMEM_LIMIT = 56 * 1024 * 1024

BF16 = jnp.bfloat16
F32 = jnp.float32
_NT = (((1,), (1,)), ((), ()))


def _params(*sem):
    return pltpu.CompilerParams(dimension_semantics=sem, vmem_limit_bytes=VMEM_LIMIT)


def _split3(x):
    hi = x.astype(BF16)
    r1 = x - hi.astype(F32)
    mid = r1.astype(BF16)
    lo = (r1 - mid.astype(F32)).astype(BF16)
    return hi, mid, lo


class Seg(NamedTuple):
    width: int
    norm: bool = False
    gain: int = 0
    n_gains: int = 1
    rope: bool = False
    scale: float = 1.0
    sigmoid: bool = False
    chunks: bool = False


def _rope(y, tab_ref):
    return y * tab_ref[0] + pltpu.roll(y, LANES // 2, 1) * tab_ref[1]


def _proj_kernel(*refs, tiles, tn, gate_sigmoid):
    x_ref, g_ref, w_ref, wg_ref, gains_ref, tab_ref, o_ref, og_ref = refs[:8]
    oc_ref = refs[8] if any(t.chunks for t in tiles) else None
    xn_ref, acc0, acc1, r_scr = refs[-4:]
    accs = (acc0, acc1)
    j = pl.program_id(1)
    sub = min(tn, 2 * LANES)

    def matmul(t):
        for c in range(tn // sub):
            accs[t % 2][:, c * sub:(c + 1) * sub] = jnp.dot(
                xn_ref[...], w_ref[:, c * sub:(c + 1) * sub], preferred_element_type=F32)

    def epilogue(t):
        seg, acc = tiles[t], accs[t % 2]
        heads = [slice(hd * LANES, (hd + 1) * LANES) for hd in range(tn // LANES)]
        if not seg.norm:
            y = acc[...]
            y = 1.0 / (1.0 + jnp.exp(-y)) if seg.sigmoid else y
            o_ref[...] = y.astype(o_ref.dtype)
            if seg.chunks:
                for hd, sl in enumerate(heads):
                    oc_ref[hd // N_KV_NSA, hd % N_KV_NSA] = y[:, sl].astype(oc_ref.dtype)
            return
        for sl in heads:
            y = acc[:, sl]
            r = lax.rsqrt(jnp.mean(y * y, axis=-1, keepdims=True) + EPS)
            r_scr[:, sl] = jnp.broadcast_to(r, y.shape)
        for hd, sl in enumerate(heads):
            gi = seg.gain + hd // (tn // LANES // seg.n_gains)
            acc[:, sl] = acc[:, sl] * r_scr[:, sl] * gains_ref[gi:gi + 1, :]
        for sl in heads:
            y = _rope(acc[:, sl], tab_ref) if seg.rope else acc[:, sl]
            if seg.scale != 1.0:
                y = y * seg.scale
            o_ref[:, sl] = y.astype(o_ref.dtype)

    @pl.when(j == 0)
    def _():
        x = x_ref[...]
        ms = jnp.mean(x * x, axis=-1, keepdims=True)
        xn_ref[...] = (x * lax.rsqrt(ms + EPS) * g_ref[...]).astype(BF16)
        gate = jnp.dot(xn_ref[...], wg_ref[...], preferred_element_type=F32)
        og_ref[...] = 1.0 / (1.0 + jnp.exp(-gate)) if gate_sigmoid else gate
        matmul(0)

    for t in range(len(tiles)):
        @pl.when(j == t + 1)
        def _(t=t):
            epilogue(t)
            if t + 1 < len(tiles):
                matmul(t + 1)


def _norm_proj(x, g, w, w_gate, gains, tabs, segs, *, seq, tm, tn, gate_sigmoid):
    T, D = x.shape
    N = w.shape[1]
    assert sum(s.width for s in segs) == N and all(s.width % tn == 0 for s in segs)
    assert T % tm == 0 and seq % tm == 0 and tn % LANES == 0
    assert all(s.norm or (not s.rope and s.scale == 1.0) for s in segs)
    assert not any(s.norm and (s.sigmoid or s.chunks) for s in segs)
    assert all(s.width == tn == 2 * N_KV_NSA * LANES for s in segs if s.chunks)
    nseq = seq // tm
    nj = N // tn
    tiles = []
    for s in segs:
        nt = s.width // tn
        assert s.n_gains == 1 or s.n_gains % nt == 0
        for k in range(nt):
            tiles.append(s._replace(width=tn, n_gains=max(s.n_gains // nt, 1),
                                    gain=s.gain + k * (s.n_gains // nt)))
    out_specs = [pl.BlockSpec((tm, tn), lambda i, j: (i, jnp.maximum(j - 1, 0))),
                 pl.BlockSpec((tm, LANES), lambda i, j: (i, 0))]
    out_shape = [jax.ShapeDtypeStruct((T, N), BF16), jax.ShapeDtypeStruct((T, LANES), F32)]
    if any(s.chunks for s in segs):
        out_specs.append(pl.BlockSpec((2, None, N_KV_NSA, tm, LANES),
                                      lambda i, j: (0, i // nseq, 0, i % nseq, 0)))
        out_shape.append(jax.ShapeDtypeStruct((2, T // seq, N_KV_NSA, seq, LANES), BF16))
    return pl.pallas_call(
        functools.partial(_proj_kernel, tiles=tuple(tiles), tn=tn, gate_sigmoid=gate_sigmoid),
        grid=(T // tm, nj + 1),
        in_specs=[
            pl.BlockSpec((tm, D), lambda i, j: (i, 0)),
            pl.BlockSpec((1, D), lambda i, j: (0, 0)),
            pl.BlockSpec((D, tn), lambda i, j: (0, jnp.minimum(j, nj - 1))),
            pl.BlockSpec((D, LANES), lambda i, j: (0, 0)),
            pl.BlockSpec(gains.shape, lambda i, j: (0, 0)),
            pl.BlockSpec((2, tm, LANES), lambda i, j: (0, i % nseq, 0)),
        ],
        out_specs=out_specs,
        out_shape=out_shape,
        scratch_shapes=[pltpu.VMEM((tm, D), BF16), pltpu.VMEM((tm, tn), F32),
                        pltpu.VMEM((tm, tn), F32), pltpu.VMEM((tm, tn), F32)],
        compiler_params=_params("parallel", "arbitrary"),
        name="norm_proj",
    )(x, g.reshape(1, D), w, w_gate, gains, tabs)


def _outproj_kernel(*refs, n_parts):
    a_refs = refs[:n_parts]
    w_refs = refs[n_parts:2 * n_parts]
    h_ref = refs[2 * n_parts]
    o_ref = refs[2 * n_parts + 1]
    acc = h_ref[...]
    for a_ref, w_ref in zip(a_refs, w_refs):
        acc = acc + jnp.dot(a_ref[...], w_ref[...], preferred_element_type=F32)
    o_ref[...] = acc


def _out_proj(parts, weights, h, *, tm, tn):
    T, D = h.shape
    n = len(parts)
    in_specs = [pl.BlockSpec((tm, p.shape[1]), lambda i, j: (i, 0)) for p in parts]
    in_specs += [pl.BlockSpec((w.shape[0], tn), lambda i, j: (0, j)) for w in weights]
    in_specs += [pl.BlockSpec((tm, tn), lambda i, j: (i, j))]
    return pl.pallas_call(
        functools.partial(_outproj_kernel, n_parts=n),
        grid=(T // tm, D // tn),
        in_specs=in_specs,
        out_specs=pl.BlockSpec((tm, tn), lambda i, j: (i, j)),
        out_shape=jax.ShapeDtypeStruct((T, D), F32),
        compiler_params=_params("parallel", "arbitrary"),
        name="out_proj",
    )(*parts, *weights, h)


def _mlp_kernel(h_ref, g_ref, wu_ref, wd_ref, o_ref, xn_ref):
    f = pl.program_id(1)

    @pl.when(f == 0)
    def _():
        x = h_ref[...]
        ms = jnp.mean(x * x, axis=-1, keepdims=True)
        xn_ref[...] = (x * lax.rsqrt(ms + EPS) * g_ref[...]).astype(BF16)
        o_ref[...] = x

    tf = wu_ref.shape[1]
    sub = min(tf, 4 * LANES)
    total = None
    for c in range(tf // sub):
        u = jnp.dot(xn_ref[...], wu_ref[:, c * sub:(c + 1) * sub], preferred_element_type=F32)
        a = jnp.square(jnp.maximum(u, 0.0)).astype(BF16)
        part = jnp.dot(a, wd_ref[c * sub:(c + 1) * sub, :], preferred_element_type=F32)
        total = part if total is None else total + part
    o_ref[...] += total


def _mlp(h, g, w_up, w_down, *, tm, tf):
    T, D = h.shape
    FF = w_up.shape[1]
    return pl.pallas_call(
        _mlp_kernel,
        grid=(T // tm, FF // tf),
        in_specs=[
            pl.BlockSpec((tm, D), lambda i, f: (i, 0)),
            pl.BlockSpec((1, D), lambda i, f: (0, 0)),
            pl.BlockSpec((D, tf), lambda i, f: (0, f)),
            pl.BlockSpec((tf, D), lambda i, f: (f, 0)),
        ],
        out_specs=pl.BlockSpec((tm, D), lambda i, f: (i, 0)),
        out_shape=jax.ShapeDtypeStruct((T, D), F32),
        scratch_shapes=[pltpu.VMEM((tm, D), BF16)],
        compiler_params=_params("parallel", "arbitrary"),
        name="sq_relu_mlp",
    )(h, g.reshape(1, D), w_up, w_down)


def _foxaug_kernel(f_ref, bf_ref, tri_ref, pq_ref, pk_ref, oq_ref, ok_ref, carry_ref, *, ts):
    @pl.when(pl.program_id(1) == 0)
    def _():
        carry_ref[...] = jnp.zeros_like(carry_ref)

    x = f_ref[...] + bf_ref[...]
    logf = jnp.minimum(x, 0.0) - jnp.log(1.0 + jnp.exp(-jnp.abs(x)))
    tri = tri_ref[...]
    c = carry_ref[...]
    for part in _split3(logf):
        c = c + jnp.dot(tri, part, preferred_element_type=F32)
    carry_ref[...] = c[ts - 1:ts, :]
    parts = jnp.concatenate(list(_split3(c * LOG2E)) + [jnp.ones((ts, LANES), BF16)], axis=1)
    oq_ref[...] = jnp.dot(parts, pq_ref[...], preferred_element_type=F32).astype(BF16)
    ok_ref[...] = jnp.dot(parts, pk_ref[...], preferred_element_type=F32).astype(BF16)


def _fox_aug(fproj, b_f, *, batch, seq, hf, ts):
    T = fproj.shape[0]
    ns = seq // ts
    tri = np.tril(np.ones((ts, ts), np.float32))
    pq = np.zeros((4 * LANES, hf * LANES), np.float32)
    pk = np.zeros((4 * LANES, hf * LANES), np.float32)
    for h in range(hf):
        for p in range(3):
            pq[p * LANES + h, h * LANES + p] = 1.0
            pq[3 * LANES, h * LANES + 3 + p] = 1.0
            pk[3 * LANES, h * LANES + p] = 1.0
            pk[p * LANES + h, h * LANES + 3 + p] = -1.0
    bf = jnp.zeros((1, LANES), F32).at[0, :hf].set(b_f)
    out = jax.ShapeDtypeStruct((T, hf * LANES), BF16)
    return pl.pallas_call(
        functools.partial(_foxaug_kernel, ts=ts),
        grid=(batch, ns),
        in_specs=[
            pl.BlockSpec((ts, LANES), lambda b, s: (b * ns + s, 0)),
            pl.BlockSpec((1, LANES), lambda b, s: (0, 0)),
            pl.BlockSpec((ts, ts), lambda b, s: (0, 0)),
            pl.BlockSpec(pq.shape, lambda b, s: (0, 0)),
            pl.BlockSpec(pk.shape, lambda b, s: (0, 0)),
        ],
        out_specs=[pl.BlockSpec((ts, hf * LANES), lambda b, s: (b * ns + s, 0))] * 2,
        out_shape=[out, out],
        scratch_shapes=[pltpu.VMEM((1, LANES), F32)],
        compiler_params=_params("parallel", "arbitrary"),
        name="fox_aug",
    )(fproj, bf, jnp.asarray(tri, BF16), jnp.asarray(pq, BF16), jnp.asarray(pk, BF16))


def _flash_kernel(*refs, nk, split, hs, tq, tk, mode, has_aug, gate_mode, gate_base, has_addin):
    it = iter(refs)
    q_ref = next(it)
    qa_ref = next(it) if has_aug else None
    k_ref = next(it)
    ka_ref = next(it) if has_aug else None
    v_ref = next(it)
    bias_ref = next(it)
    gate_ref = next(it) if gate_mode else None
    add_ref = next(it) if has_addin else None
    o_ref = next(it)
    scr = list(it)
    per = 1 if mode == "window" else 4
    nch = nk * split
    hc = hs // split
    chains = [scr[c * per:(c + 1) * per] for c in range(nch)]

    qi = pl.program_id(2)
    M = hc * tq
    ones = jnp.ones((tk, LANES), BF16)
    lanes = lambda c: slice((c // split) * LANES, (c // split + 1) * LANES)
    head0 = lambda c: (c // split) * hs + (c % split) * hc

    for c in range(nch):
        q_scr = chains[c][0]
        for h in range(hc):
            col = (head0(c) + h) * LANES
            q_scr[h * tq:(h + 1) * tq, :LANES] = q_ref[:, col:col + LANES]
            if has_aug:
                q_scr[h * tq:(h + 1) * tq, LANES:] = qa_ref[:, lanes(c)]

    def scores(c, k0):
        k = k_ref[pl.ds(k0, tk), lanes(c)]
        if has_aug:
            k = jnp.concatenate([k, ka_ref[pl.ds(k0, tk), lanes(c)]], axis=1)
        return lax.dot_general(chains[c][0][...], k, _NT, preferred_element_type=F32)

    def biased(s, off):
        b = bias_ref[off]
        return (s.reshape(hc, tq, tk) + b[None]).reshape(M, tk) if hc > 1 else s + b

    def values(c, k0):
        return jnp.concatenate([v_ref[pl.ds(k0, tk), lanes(c)], ones], axis=1)

    accs = []
    if mode == "window":
        back = tk - tq
        k0 = pl.multiple_of(jnp.maximum(qi * tq - back, 0), tq)
        for c in range(nch):
            s = biased(scores(c, k0), jnp.minimum(qi, back // tq))
            p = jnp.exp2(s - jnp.max(s, axis=1, keepdims=True))
            accs.append(jnp.dot(p.astype(BF16), values(c, k0), preferred_element_type=F32))
    else:
        for c in range(nch):
            _, m_scr, acc_scr, _ = chains[c]
            m_scr[...] = jnp.full(m_scr.shape, NEG, F32)
            acc_scr[...] = jnp.zeros(acc_scr.shape, F32)

        def stage(c, kt):
            chains[c][3][kt & 1] = scores(c, pl.multiple_of(kt * tk, tk))

        def update(c, kt, off):
            _, m_scr, acc_scr, s_scr = chains[c]
            s = s_scr[kt & 1]
            if off is not None:
                s = biased(s, off)
            m_prev = m_scr[...]
            m_new = jnp.maximum(m_prev, jnp.max(s, axis=1, keepdims=True))
            alpha = jnp.exp2(m_prev - m_new)
            p = jnp.exp2(s - jnp.concatenate([m_new] * (tk // LANES), axis=1))
            pv = jnp.dot(p.astype(BF16), values(c, pl.multiple_of(kt * tk, tk)),
                         preferred_element_type=F32)
            acc_scr[...] = jnp.concatenate([alpha, alpha], axis=1) * acc_scr[...] + pv
            m_scr[...] = m_new

        if mode == "causal":
            last = (qi * tq) // tk
            first, last_off = 0, qi - last * (tk // tq)
        else:
            last = qi
            first, last_off = jnp.maximum(qi - (bias_ref.shape[0] - 1), 0), 0

        for c in range(nch):
            stage(c, first)

        def body(kt, carry):
            for c in range(nch):
                update(c, kt, None if mode == "causal" else qi - kt)
                stage(c, kt + 1)
            return carry

        lax.fori_loop(first, last, body, 0)
        for c in range(nch):
            update(c, last, last_off)
            accs.append(chains[c][2][...])

    for c in range(nch):
        o = accs[c][:, :LANES] * (1.0 / accs[c][:, LANES:])
        for h in range(hc):
            oh = o[h * tq:(h + 1) * tq, :]
            head = head0(c) + h
            sl = slice(head * LANES, (head + 1) * LANES)
            if gate_mode == "elem":
                oh = oh * gate_ref[:, sl].astype(F32)
            elif gate_mode == "col":
                idx = (pl.program_id(1) * hs + head) * 3 + gate_base
                lane = lax.broadcasted_iota(jnp.int32, (tq, LANES), 1)
                oh = oh * jnp.sum(jnp.where(lane == idx, gate_ref[...], 0.0), axis=1, keepdims=True)
            if has_addin:
                oh = oh + add_ref[:, sl]
            o_ref[:, sl] = oh.astype(o_ref.dtype)


def _flash(q_arr, q_col0, k_arr, k_col0, v_arr, v_col0, bias, *, n_kv, hs, tq, tk, mode, out_dtype,
           nk=1, split=1, qaug=None, kaug=None, gate=None, gate_mode=None, gate_col0=0, gate_base=0,
           addin=None):
    B, S, _ = q_arr.shape
    qw = hs * LANES
    assert q_col0 % (nk * qw) == 0 and k_col0 % (nk * LANES) == 0 and v_col0 % (nk * LANES) == 0
    assert S % tq == 0 and S % tk == 0 if mode != "window" else (S % tq == 0 and tk <= S)
    assert tk % tq == 0 and (mode != "band" or tq == tk) and bias.shape[1:] == (tq, tk)
    assert n_kv % nk == 0 and hs % split == 0
    assert nk == 1 or (gate_mode != "col" and (qaug is None or qaug.ndim == 3))
    qb, kb, vb = q_col0 // (nk * qw), k_col0 // (nk * LANES), v_col0 // (nk * LANES)
    assert (qaug is None) == (kaug is None)
    dk = LANES if qaug is None else 2 * LANES
    ins, specs = [q_arr], [pl.BlockSpec((None, tq, nk * qw), lambda b, h, i: (b, i, qb + h))]
    if qaug is not None:
        ins.append(qaug)
        if qaug.ndim == 3:
            specs.append(pl.BlockSpec((None, tq, nk * LANES), lambda b, h, i: (b, i, h)))
        else:
            specs.append(pl.BlockSpec((None, None, tq, LANES), lambda b, h, i: (b, h, i, 0)))
    ins.append(k_arr)
    specs.append(pl.BlockSpec((None, S, nk * LANES), lambda b, h, i: (b, 0, kb + h)))
    if kaug is not None:
        ins.append(kaug)
        if kaug.ndim == 3:
            specs.append(pl.BlockSpec((None, S, nk * LANES), lambda b, h, i: (b, 0, h)))
        else:
            specs.append(pl.BlockSpec((S, LANES), lambda b, h, i: (0, 0)))
    ins.append(v_arr)
    specs.append(pl.BlockSpec((None, S, nk * LANES), lambda b, h, i: (b, 0, vb + h)))
    ins.append(bias)
    specs.append(pl.BlockSpec(bias.shape, lambda b, h, i: (0, 0, 0)))
    if gate_mode == "elem":
        assert gate_col0 % (nk * LANES) == 0
        gb = gate_col0 // (nk * LANES)
        ins.append(gate)
        specs.append(pl.BlockSpec((None, tq, nk * LANES), lambda b, h, i: (b, i, gb + h)))
    elif gate_mode == "col":
        ins.append(gate)
        specs.append(pl.BlockSpec((None, tq, LANES), lambda b, h, i: (b, i, 0)))
    if addin is not None:
        ins.append(addin)
        specs.append(pl.BlockSpec((None, tq, nk * qw), lambda b, h, i: (b, i, h)))
    kern = functools.partial(
        _flash_kernel, nk=nk, split=split, hs=hs, tq=tq, tk=tk, mode=mode, has_aug=qaug is not None,
        gate_mode=gate_mode, gate_base=gate_base, has_addin=addin is not None)
    mc = hs // split * tq
    scratch = []
    for _ in range(nk * split):
        scratch.append(pltpu.VMEM((mc, dk), BF16))
        if mode != "window":
            scratch += [pltpu.VMEM((mc, LANES), F32), pltpu.VMEM((mc, 2 * LANES), F32),
                        pltpu.VMEM((2, mc, tk), F32)]
    return pl.pallas_call(
        kern,
        grid=(B, n_kv // nk, S // tq),
        in_specs=specs,
        out_specs=pl.BlockSpec((None, tq, nk * qw), lambda b, h, i: (b, i, h)),
        out_shape=jax.ShapeDtypeStruct((B, S, n_kv * qw), out_dtype),
        scratch_shapes=scratch,
        compiler_params=_params("parallel", "parallel", "arbitrary"),
        name="flash_attention",
    )(*ins)


def _bias_table(tq, tk, n, weight_of_distance):
    r = np.arange(tq)[:, None]
    c = np.arange(tk)[None, :]
    tabs = []
    for v in range(n):
        w = weight_of_distance(v * tq + r - c)
        tabs.append(np.where(w > 0, np.log2(np.maximum(w, 1.0)), NEG))
    return jnp.asarray(np.stack(tabs), F32)


def _dilated_weight(d):
    w = np.zeros(d.shape, np.float64)
    for window, dil in DILATED_PATTERNS:
        w += (d >= 0) & (d <= window) & (d % dil == 0)
    return w


def _compress_kernel(x_ref, pe_ref, w1_ref, w2_ref, g_ref, tab_ref, o_ref, *, nc):
    half = CMP_STRIDE * HEAD_DIM
    x = x_ref[...].astype(F32)
    top = (x + pe_ref[:, :half]).astype(BF16)
    bot = (x + pe_ref[:, half:]).astype(BF16)
    a = jnp.dot(top, w1_ref[:half, :], preferred_element_type=F32)
    b = jnp.dot(bot, w1_ref[half:, :], preferred_element_type=F32)
    pre = a + pltpu.roll(b, nc - 1, 0)
    hid = pre * (0.5 * (1.0 + jnp.tanh(np.sqrt(2.0 / np.pi) * (pre + 0.044715 * (pre * pre * pre)))))
    y = jnp.dot(hid.astype(BF16), w2_ref[...], preferred_element_type=F32)

    @pl.when(pl.program_id(0) == 0)
    def _():
        ms = jnp.mean(y * y, axis=-1, keepdims=True)
        o_ref[...] = _rope(y * lax.rsqrt(ms + EPS) * g_ref[...], tab_ref).astype(BF16)

    @pl.when(pl.program_id(0) != 0)
    def _():
        o_ref[...] = y.astype(BF16)


def _compress(x2, pe, w1, w2, g_kc, tabs_c):
    _, BG, nc, width = x2.shape
    return pl.pallas_call(
        functools.partial(_compress_kernel, nc=nc),
        grid=(2, BG),
        in_specs=[
            pl.BlockSpec((None, None, nc, width), lambda s, i: (s, i, 0, 0)),
            pl.BlockSpec((None, 1, 2 * width), lambda s, i: (s, 0, 0)),
            pl.BlockSpec((None, 2 * width, HEAD_DIM), lambda s, i: (s, 0, 0)),
            pl.BlockSpec((None, HEAD_DIM, HEAD_DIM), lambda s, i: (s, 0, 0)),
            pl.BlockSpec((1, HEAD_DIM), lambda s, i: (0, 0)),
            pl.BlockSpec((2, nc, LANES), lambda s, i: (0, 0, 0)),
        ],
        out_specs=pl.BlockSpec((None, None, nc, HEAD_DIM), lambda s, i: (s, i, 0, 0)),
        out_shape=jax.ShapeDtypeStruct((2, BG, nc, HEAD_DIM), BF16),
        compiler_params=_params("arbitrary", "arbitrary"),
        name="nsa_compress",
    )(x2, pe, w1, w2, g_kc.reshape(1, HEAD_DIM), tabs_c)


def _cmp_topk_kernel(q_ref, kc_ref, vc_ref, gate_ref, ov_ref, eye_ref, oc_ref, sel_ref, v_scr, *,
                     hs, t, nc, n_slc, n_sel):
    g = pl.program_id(1)
    q0 = pl.program_id(2) * t
    M = hs * t
    q = jnp.concatenate([q_ref[:, h * LANES:(h + 1) * LANES] for h in range(hs)], axis=0)
    s = lax.dot_general(q, kc_ref[...], _NT, preferred_element_type=F32)
    qpos = q0 + lax.broadcasted_iota(jnp.int32, (t, nc), 0)
    n = lax.broadcasted_iota(jnp.int32, (t, nc), 1)
    mask = n * CMP_STRIDE + (CMP_BLOCK - 1) <= qpos
    keep = jnp.where(mask, 1.0, 0.0)[None]
    s = s.reshape(hs, t, nc) + jnp.where(mask, 0.0, NEG)[None]
    m = jnp.max(s, axis=2, keepdims=True)
    e = jnp.exp2(s - m) * keep
    p = (e / jnp.maximum(jnp.sum(e, axis=2, keepdims=True), 1e-30)).reshape(M, nc)
    oc = jnp.dot(p.astype(BF16), vc_ref[...], preferred_element_type=F32)
    lane = lax.broadcasted_iota(jnp.int32, (t, LANES), 1)
    gates = gate_ref[...]
    psum = jnp.zeros((t, nc), F32)
    for h in range(hs):
        idx = (g * hs + h) * 3
        gcol = jnp.sum(jnp.where(lane == idx, gates, 0.0), axis=1, keepdims=True)
        oc_ref[:, h * LANES:(h + 1) * LANES] = oc[h * t:(h + 1) * t, :] * gcol
        psum = psum + p[h * t:(h + 1) * t, :]

    imp = jnp.zeros((LANES, t), F32)
    for part in _split3(psum):
        imp = imp + lax.dot_general(ov_ref[...], part, _NT, preferred_element_type=F32)
    jrow = lax.broadcasted_iota(jnp.int32, (LANES, t), 0)
    cur = (q0 + lax.broadcasted_iota(jnp.int32, (LANES, t), 1)) // SLC_BLOCK
    forced = (jrow == 0) | (jrow == cur) | (jrow == cur - 1)
    val = jnp.where(forced, FORCED_SCORE, imp)
    val = jnp.where(jrow <= cur, val, NEG)
    v_scr[...] = val

    n_grp = -(-n_slc // 8)
    sub8 = lax.broadcasted_iota(jnp.int32, (8, t), 0)
    vals = [val[8 * k:8 * k + 8, :] for k in range(n_grp)]
    ranks = [jnp.zeros((8, t), F32) for _ in range(n_grp)]
    for i in range(n_slc):
        vi = jnp.broadcast_to(v_scr[i:i + 1, :], (8, t))
        for k in range(n_grp):
            ge = jnp.where(vi >= vals[k], 1.0, 0.0)
            gt = jnp.where(vi > vals[k], 1.0, 0.0)
            if 8 * k > i:
                inc = ge
            elif 8 * k + 7 < i:
                inc = gt
            else:
                inc = jnp.where(sub8 > i - 8 * k, ge, gt)
            ranks[k] = ranks[k] + inc
    rank = jnp.concatenate(ranks + [jnp.zeros((LANES - 8 * n_grp, t), F32)], axis=0)
    sel = (rank < n_sel) & (jrow <= cur)
    selb = jnp.where(sel | (jrow >= n_slc), 0.0, NEG).astype(BF16)
    sel_ref[...] = lax.dot_general(eye_ref[...], selb, _NT, preferred_element_type=F32).astype(BF16)


def _cmp_topk(q_arr, kvc, gates, *, batch, seq, hs, t):
    G = N_KV_NSA
    nc = kvc.shape[2]
    n_slc = seq // SLC_BLOCK
    assert n_slc <= LANES and nc == seq // CMP_STRIDE
    qw = hs * LANES
    start = np.arange(nc) * CMP_STRIDE
    js = np.arange(LANES) * SLC_BLOCK
    ov = ((start[None, :] < js[:, None] + SLC_BLOCK) & (start[None, :] + CMP_BLOCK > js[:, None]))
    ov = ov & (np.arange(nc)[None, :] < nc - 1) & (np.arange(LANES)[:, None] < n_slc)
    kern = functools.partial(_cmp_topk_kernel, hs=hs, t=t, nc=nc, n_slc=n_slc,
                             n_sel=min(N_SELECT, n_slc))
    return pl.pallas_call(
        kern,
        grid=(batch, G, seq // t),
        in_specs=[
            pl.BlockSpec((None, t, qw), lambda b, g, i: (b, i, g)),
            pl.BlockSpec((None, None, nc, HEAD_DIM), lambda b, g, i: (0, b * G + g, 0, 0)),
            pl.BlockSpec((None, None, nc, HEAD_DIM), lambda b, g, i: (1, b * G + g, 0, 0)),
            pl.BlockSpec((None, t, LANES), lambda b, g, i: (b, i, 0)),
            pl.BlockSpec((LANES, nc), lambda b, g, i: (0, 0)),
            pl.BlockSpec((t, t), lambda b, g, i: (0, 0)),
        ],
        out_specs=[
            pl.BlockSpec((None, t, qw), lambda b, g, i: (b, i, g)),
            pl.BlockSpec((None, None, t, LANES), lambda b, g, i: (b, g, i, 0)),
        ],
        out_shape=[jax.ShapeDtypeStruct((batch, seq, G * qw), F32),
                   jax.ShapeDtypeStruct((batch, G, seq, LANES), BF16)],
        scratch_shapes=[pltpu.VMEM((LANES, t), F32)],
        compiler_params=_params("parallel", "parallel", "arbitrary"),
        name="nsa_cmp_topk",
    )(q_arr, kvc, kvc, gates, jnp.asarray(ov, BF16), jnp.eye(t, dtype=BF16))


def _permute_heads(w):
    half, mid = ROPE_DIM // 2, LANES // 2
    h = w.reshape(w.shape[:-1] + (-1, LANES))
    h = jnp.concatenate([h[..., :half], h[..., mid:mid + half], h[..., ROPE_DIM:mid],
                         h[..., half:ROPE_DIM], h[..., mid + half:]], axis=-1)
    return h.reshape(w.shape)


def _rope_tabs(pos):
    half = ROPE_DIM // 2
    inv = 1.0 / (ROPE_THETA ** (jnp.arange(0, ROPE_DIM, 2, dtype=F32) / ROPE_DIM))
    ang = pos.astype(F32)[:, None] * inv
    cos, sin = jnp.cos(ang), jnp.sin(ang)
    n = pos.shape[0]
    second = slice(LANES // 2, LANES // 2 + half)
    c = jnp.ones((n, LANES), F32).at[:, :half].set(cos).at[:, second].set(cos)
    s = jnp.zeros((n, LANES), F32).at[:, :half].set(-sin).at[:, second].set(sin)
    return jnp.stack([c, s])


class Tiles(NamedTuple):
    mlp_rows: int
    proj_rows: int
    fox: int
    dilated: int
    select: int
    window_rows: int
    compressed_rows: int


MLP_FF_TILE = 1024
PROJ_COLS = 1024


def _tiles(seq):
    fit = lambda t: min(t, seq)
    tl = Tiles(mlp_rows=fit(512), proj_rows=fit(1024), fox=fit(512), dilated=fit(512),
               select=fit(512), window_rows=fit(256), compressed_rows=fit(256))
    assert all(seq % t == 0 for t in tl)
    return tl


def _pad_rows(g, rows=8):
    return jnp.zeros((rows, g.shape[-1]), F32).at[:g.shape[0]].set(g)


def _pad_cols(w, n):
    return jnp.zeros((w.shape[0], n), w.dtype).at[:, :w.shape[1]].set(w)


def kernel(x, ln_mix_g, ln_mlp_g, w_mlp_up, w_mlp_down, even_w_in, even_b_f, even_w_out,
           even_g_q_fox, even_g_k_fox, even_g_q_dil, even_g_k_dil, odd_w_in, odd_w_out,
           odd_phi_k_pe, odd_phi_k_w1, odd_phi_k_w2, odd_phi_v_pe, odd_phi_v_w1, odd_phi_v_w2,
           odd_g_q, odd_g_kc, odd_g_ks, odd_g_kw):
    B, S, D = x.shape
    T = B * S
    n_heads = D // HEAD_DIM
    hf = n_heads // 2
    hd = n_heads - hf
    fw, dw = hf * HEAD_DIM, hd * HEAD_DIM
    G = N_KV_NSA
    hpg = n_heads // G
    qw, kvw = n_heads * HEAD_DIM, G * HEAD_DIM
    depth = ln_mix_g.shape[0]

    tl = _tiles(S)
    tm, tm_proj, t_fox, t_band = tl.mlp_rows, tl.proj_rows, tl.fox, tl.window_rows
    tk_sel, t_dil, t_cmp = tl.select, tl.dilated, tl.compressed_rows
    tabs = _rope_tabs(jnp.arange(S))
    nc = S // CMP_STRIDE
    tabs_c = _rope_tabs(jnp.arange(nc) * CMP_STRIDE + CMP_BLOCK - 1)

    causal_w = lambda d: (d >= 0).astype(np.float64)
    win_w = lambda d: ((d >= 0) & (d < WINDOW_NSA)).astype(np.float64)
    dil_span = max(w for w, _ in DILATED_PATTERNS)
    fox_bias = _bias_table(t_fox, t_fox, 1, causal_w)
    dil_bias = _bias_table(t_dil, t_dil, (dil_span + t_dil - 1) // t_dil + 1, _dilated_weight)
    sel_bias = _bias_table(tk_sel, tk_sel, 1, causal_w)
    win_back = -(-(WINDOW_NSA - 1) // t_band) * t_band
    win_bias = _bias_table(t_band, win_back + t_band, win_back // t_band + 1, win_w)
    onehot_blk = jnp.asarray(
        (np.arange(S)[:, None] // SLC_BLOCK == np.arange(LANES)[None, :]).astype(np.float32), BF16)

    h = x.reshape(T, D)
    for layer in range(depth):
        i = layer // 2
        if layer % 2 == 0:
            w_in = even_w_in[i]
            c_qd = 4 * fw + hf
            w_main = jnp.concatenate(
                [w_in[:, :4 * fw], _permute_heads(w_in[:, c_qd:c_qd + 2 * dw]),
                 w_in[:, c_qd + 2 * dw:]], axis=1).astype(BF16)
            w_f = _pad_cols(w_in[:, 4 * fw:4 * fw + hf], LANES).astype(BF16)
            gains = _pad_rows(jnp.stack([even_g_q_fox[i], even_g_k_fox[i],
                                         _permute_heads(even_g_q_dil[i]),
                                         _permute_heads(even_g_k_dil[i])]))
            segs = [Seg(fw, norm=True, gain=0, scale=QSCALE), Seg(fw, norm=True, gain=1), Seg(fw),
                    Seg(fw, sigmoid=True),
                    Seg(dw, norm=True, gain=2, rope=True, scale=QSCALE),
                    Seg(dw, norm=True, gain=3, rope=True), Seg(dw)]
            proj, fproj = _norm_proj(h, ln_mix_g[layer], w_main, w_f, gains, tabs, segs, seq=S,
                                     tm=tm_proj, tn=min(PROJ_COLS, fw, dw), gate_sigmoid=False)
            proj = proj.reshape(B, S, -1)
            qaug, kaug = _fox_aug(fproj, even_b_f[i], batch=B, seq=S, hf=hf, ts=t_fox)
            o_a = _flash(proj, 0, proj, fw, proj, 2 * fw, fox_bias, n_kv=hf, hs=1, tq=t_fox,
                         tk=t_fox, mode="causal", nk=min(4, hf), out_dtype=BF16,
                         qaug=qaug.reshape(B, S, -1),
                         kaug=kaug.reshape(B, S, -1), gate=proj, gate_mode="elem",
                         gate_col0=3 * fw)
            o_b = _flash(proj, 4 * fw, proj, 4 * fw + dw, proj, 4 * fw + 2 * dw, dil_bias, n_kv=hd,
                         hs=1, tq=t_dil, tk=t_dil, mode="band", nk=min(4, hd), out_dtype=BF16)
            w_out = even_w_out[i].astype(BF16)
            h = _out_proj([o_a.reshape(T, fw), o_b.reshape(T, dw)], [w_out[:fw], w_out[fw:]], h,
                          tm=tm_proj, tn=min(PROJ_COLS, D))
        else:
            w_in = odd_w_in[i]
            n_main = qw + 6 * kvw
            c_kc, c_vc, c_ks, c_kw, c_vs, c_vw = (qw + n * kvw for n in range(6))
            w_main = jnp.concatenate(
                [_permute_heads(w_in[:, :qw]), w_in[:, qw:qw + 2 * kvw],
                 _permute_heads(w_in[:, qw + 2 * kvw:qw + 3 * kvw]),
                 _permute_heads(w_in[:, qw + 4 * kvw:qw + 5 * kvw]),
                 w_in[:, qw + 3 * kvw:qw + 4 * kvw], w_in[:, qw + 5 * kvw:n_main]],
                axis=1).astype(BF16)
            w_g = _pad_cols(w_in[:, n_main:], LANES).astype(BF16)
            gains = _pad_rows(_permute_heads(jnp.stack([odd_g_q[i], odd_g_ks[i], odd_g_kw[i]])))
            segs = [Seg(qw, norm=True, gain=0, rope=True, scale=QSCALE), Seg(2 * kvw, chunks=True),
                    Seg(2 * kvw, norm=True, gain=1, n_gains=2, rope=True), Seg(2 * kvw)]
            proj, gates, kvraw = _norm_proj(h, ln_mix_g[layer], w_main, w_g, gains, tabs, segs,
                                            seq=S, tm=tm_proj, tn=min(PROJ_COLS, 2 * kvw),
                                            gate_sigmoid=True)
            proj, gates = proj.reshape(B, S, -1), gates.reshape(B, S, LANES)
            x2 = kvraw.reshape(2, B * G, nc, CMP_STRIDE * HEAD_DIM)
            pe = jnp.stack([odd_phi_k_pe[i], odd_phi_v_pe[i]]).reshape(2, 1, CMP_BLOCK * HEAD_DIM)
            w1 = jnp.stack([odd_phi_k_w1[i], odd_phi_v_w1[i]]).astype(BF16)
            w2 = jnp.stack([_permute_heads(odd_phi_k_w2[i]), odd_phi_v_w2[i]]).astype(BF16)
            kvc = _compress(x2, pe, w1, w2, _permute_heads(odd_g_kc[i]), tabs_c)
            o_c, selb = _cmp_topk(proj, kvc, gates, batch=B, seq=S, hs=hpg, t=t_cmp)
            o_cs = _flash(proj, 0, proj, c_ks, proj, c_vs, sel_bias, n_kv=G,
                          hs=hpg, tq=tk_sel, tk=tk_sel, mode="causal", split=hpg, out_dtype=F32,
                          qaug=selb,
                          kaug=onehot_blk, gate=gates, gate_mode="col", gate_base=1, addin=o_c)
            o = _flash(proj, 0, proj, c_kw, proj, c_vw, win_bias, n_kv=G, hs=hpg,
                       tq=t_band, tk=win_back + t_band, mode="window", split=hpg, out_dtype=BF16,
                       gate=gates,
                       gate_mode="col", gate_base=2, addin=o_cs)
            h = _out_proj([o.reshape(T, qw)], [odd_w_out[i].astype(BF16)], h, tm=tm_proj, tn=min(PROJ_COLS, D))
        h = _mlp(h, ln_mlp_g[layer], w_mlp_up[layer].astype(BF16), w_mlp_down[layer].astype(BF16),
                 tm=tm, tf=MLP_FF_TILE)
    return h.reshape(B, S, D)
```

```python
import functools
from typing import NamedTuple

import numpy as np
import jax
import jax.numpy as jnp
from jax import lax
from jax.experimental import pallas as pl
from jax.experimental.pallas import tpu as pltpu

HEAD_DIM = 128
LANES = 128
N_KV_NSA = 4
DILATED_PATTERNS = ((128, 1), (512, 4), (2048, 16))
CMP_BLOCK = 32
CMP_STRIDE = 16
SLC_BLOCK = 64
N_SELECT = 16
WINDOW_NSA = 512
ROPE_THETA = 500000.0
ROPE_DIM = HEAD_DIM // 4
EPS = 1e-6
NEG = -1e30
FORCED_SCORE = 1e9
SCALE = HEAD_DIM ** -0.5
LOG2E = float(np.log2(np.e))
QSCALE = SCALE * LOG2E
VMEM_LIMIT = 56 * 1024 * 1024

BF16 = jnp.bfloat16
F32 = jnp.float32
_NT = (((1,), (1,)), ((), ()))


def _params(*sem):
    return pltpu.CompilerParams(dimension_semantics=sem, vmem_limit_bytes=VMEM_LIMIT)


def _split3(x):
    hi = x.astype(BF16)
    r1 = x - hi.astype(F32)
    mid = r1.astype(BF16)
    lo = (r1 - mid.astype(F32)).astype(BF16)
    return hi, mid, lo


class Seg(NamedTuple):
    width: int
    norm: bool = False
    gain: int = 0
    n_gains: int = 1
    rope: bool = False
    scale: float = 1.0
    sigmoid: bool = False
    chunks: bool = False


def _rope(y, tab_ref):
    half = ROPE_DIM // 2
    return (y * tab_ref[0] + pltpu.roll(y, LANES - half, 1) * tab_ref[1]
            + pltpu.roll(y, half, 1) * tab_ref[2])


def _proj_kernel(*refs, tiles, tn, gate_sigmoid):
    x_ref, g_ref, w_ref, wg_ref, gains_ref, tab_ref, o_ref, og_ref = refs[:8]
    oc_ref = refs[8] if any(t.chunks for t in tiles) else None
    xn_ref, acc0, acc1, r_scr = refs[-4:]
    accs = (acc0, acc1)
    j = pl.program_id(1)
    sub = min(tn, 2 * LANES)

    def matmul(t):
        for c in range(tn // sub):
            accs[t % 2][:, c * sub:(c + 1) * sub] = jnp.dot(
                xn_ref[...], w_ref[:, c * sub:(c + 1) * sub], preferred_element_type=F32)

    def epilogue(t):
        seg, acc = tiles[t], accs[t % 2]
        heads = [slice(hd * LANES, (hd + 1) * LANES) for hd in range(tn // LANES)]
        if not seg.norm:
            y = acc[...]
            y = 1.0 / (1.0 + jnp.exp(-y)) if seg.sigmoid else y
            o_ref[...] = y.astype(o_ref.dtype)
            if seg.chunks:
                for hd, sl in enumerate(heads):
                    oc_ref[hd // N_KV_NSA, hd % N_KV_NSA] = y[:, sl].astype(oc_ref.dtype)
            return
        for sl in heads:
            y = acc[:, sl]
            r = lax.rsqrt(jnp.mean(y * y, axis=-1, keepdims=True) + EPS)
            r_scr[:, sl] = jnp.broadcast_to(r, y.shape)
        for hd, sl in enumerate(heads):
            gi = seg.gain + hd // (tn // LANES // seg.n_gains)
            acc[:, sl] = acc[:, sl] * r_scr[:, sl] * gains_ref[gi:gi + 1, :]
        for sl in heads:
            y = _rope(acc[:, sl], tab_ref) if seg.rope else acc[:, sl]
            if seg.scale != 1.0:
                y = y * seg.scale
            o_ref[:, sl] = y.astype(o_ref.dtype)

    @pl.when(j == 0)
    def _():
        x = x_ref[...]
        ms = jnp.mean(x * x, axis=-1, keepdims=True)
        xn_ref[...] = (x * lax.rsqrt(ms + EPS) * g_ref[...]).astype(BF16)
        gate = jnp.dot(xn_ref[...], wg_ref[...], preferred_element_type=F32)
        og_ref[...] = 1.0 / (1.0 + jnp.exp(-gate)) if gate_sigmoid else gate
        matmul(0)

    for t in range(len(tiles)):
        @pl.when(j == t + 1)
        def _(t=t):
            epilogue(t)
            if t + 1 < len(tiles):
                matmul(t + 1)


def _norm_proj(x, g, w, w_gate, gains, tabs, segs, *, seq, tm, tn, gate_sigmoid):
    T, D = x.shape
    N = w.shape[1]
    assert sum(s.width for s in segs) == N and all(s.width % tn == 0 for s in segs)
    assert T % tm == 0 and seq % tm == 0 and tn % LANES == 0
    assert all(s.norm or (not s.rope and s.scale == 1.0) for s in segs)
    assert not any(s.norm and (s.sigmoid or s.chunks) for s in segs)
    assert all(s.width == tn == 2 * N_KV_NSA * LANES for s in segs if s.chunks)
    nseq = seq // tm
    nj = N // tn
    tiles = []
    for s in segs:
        nt = s.width // tn
        assert s.n_gains == 1 or s.n_gains % nt == 0
        for k in range(nt):
            tiles.append(s._replace(width=tn, n_gains=max(s.n_gains // nt, 1),
                                    gain=s.gain + k * (s.n_gains // nt)))
    out_specs = [pl.BlockSpec((tm, tn), lambda i, j: (i, jnp.maximum(j - 1, 0))),
                 pl.BlockSpec((tm, LANES), lambda i, j: (i, 0))]
    out_shape = [jax.ShapeDtypeStruct((T, N), BF16), jax.ShapeDtypeStruct((T, LANES), F32)]
    if any(s.chunks for s in segs):
        out_specs.append(pl.BlockSpec((2, None, N_KV_NSA, tm, LANES),
                                      lambda i, j: (0, i // nseq, 0, i % nseq, 0)))
        out_shape.append(jax.ShapeDtypeStruct((2, T // seq, N_KV_NSA, seq, LANES), BF16))
    return pl.pallas_call(
        functools.partial(_proj_kernel, tiles=tuple(tiles), tn=tn, gate_sigmoid=gate_sigmoid),
        grid=(T // tm, nj + 1),
        in_specs=[
            pl.BlockSpec((tm, D), lambda i, j: (i, 0)),
            pl.BlockSpec((1, D), lambda i, j: (0, 0)),
            pl.BlockSpec((D, tn), lambda i, j: (0, jnp.minimum(j, nj - 1))),
            pl.BlockSpec((D, LANES), lambda i, j: (0, 0)),
            pl.BlockSpec(gains.shape, lambda i, j: (0, 0)),
            pl.BlockSpec((3, tm, LANES), lambda i, j: (0, i % nseq, 0)),
        ],
        out_specs=out_specs,
        out_shape=out_shape,
        scratch_shapes=[pltpu.VMEM((tm, D), BF16), pltpu.VMEM((tm, tn), F32),
                        pltpu.VMEM((tm, tn), F32), pltpu.VMEM((tm, tn), F32)],
        compiler_params=_params("parallel", "arbitrary"),
        name="norm_proj",
    )(x, g.reshape(1, D), w, w_gate, gains, tabs)


def _outproj_kernel(*refs, n_parts):
    a_refs = refs[:n_parts]
    w_refs = refs[n_parts:2 * n_parts]
    h_ref = refs[2 * n_parts]
    o_ref = refs[2 * n_parts + 1]
    acc = h_ref[...]
    for a_ref, w_ref in zip(a_refs, w_refs):
        acc = acc + jnp.dot(a_ref[...], w_ref[...], preferred_element_type=F32)
    o_ref[...] = acc


def _out_proj(parts, weights, h, *, tm, tn):
    T, D = h.shape
    n = len(parts)
    in_specs = [pl.BlockSpec((tm, p.shape[1]), lambda i, j: (i, 0)) for p in parts]
    in_specs += [pl.BlockSpec((w.shape[0], tn), lambda i, j: (0, j)) for w in weights]
    in_specs += [pl.BlockSpec((tm, tn), lambda i, j: (i, j))]
    return pl.pallas_call(
        functools.partial(_outproj_kernel, n_parts=n),
        grid=(T // tm, D // tn),
        in_specs=in_specs,
        out_specs=pl.BlockSpec((tm, tn), lambda i, j: (i, j)),
        out_shape=jax.ShapeDtypeStruct((T, D), F32),
        compiler_params=_params("parallel", "arbitrary"),
        name="out_proj",
    )(*parts, *weights, h)


def _mlp_kernel(h_ref, g_ref, wu_ref, wd_ref, o_ref, xn_ref):
    f = pl.program_id(1)

    @pl.when(f == 0)
    def _():
        x = h_ref[...]
        ms = jnp.mean(x * x, axis=-1, keepdims=True)
        xn_ref[...] = (x * lax.rsqrt(ms + EPS) * g_ref[...]).astype(BF16)
        o_ref[...] = x

    tf = wu_ref.shape[1]
    sub = min(tf, 4 * LANES)
    total = None
    for c in range(tf // sub):
        u = jnp.dot(xn_ref[...], wu_ref[:, c * sub:(c + 1) * sub], preferred_element_type=F32)
        a = jnp.square(jnp.maximum(u, 0.0)).astype(BF16)
        part = jnp.dot(a, wd_ref[c * sub:(c + 1) * sub, :], preferred_element_type=F32)
        total = part if total is None else total + part
    o_ref[...] += total


def _mlp(h, g, w_up, w_down, *, tm, tf):
    T, D = h.shape
    FF = w_up.shape[1]
    return pl.pallas_call(
        _mlp_kernel,
        grid=(T // tm, FF // tf),
        in_specs=[
            pl.BlockSpec((tm, D), lambda i, f: (i, 0)),
            pl.BlockSpec((1, D), lambda i, f: (0, 0)),
            pl.BlockSpec((D, tf), lambda i, f: (0, f)),
            pl.BlockSpec((tf, D), lambda i, f: (f, 0)),
        ],
        out_specs=pl.BlockSpec((tm, D), lambda i, f: (i, 0)),
        out_shape=jax.ShapeDtypeStruct((T, D), F32),
        scratch_shapes=[pltpu.VMEM((tm, D), BF16)],
        compiler_params=_params("parallel", "arbitrary"),
        name="sq_relu_mlp",
    )(h, g.reshape(1, D), w_up, w_down)


def _foxaug_kernel(f_ref, bf_ref, tri_ref, pq_ref, pk_ref, oq_ref, ok_ref, carry_ref, *, ts):
    @pl.when(pl.program_id(1) == 0)
    def _():
        carry_ref[...] = jnp.zeros_like(carry_ref)

    x = f_ref[...] + bf_ref[...]
    logf = jnp.minimum(x, 0.0) - jnp.log(1.0 + jnp.exp(-jnp.abs(x)))
    tri = tri_ref[...]
    c = carry_ref[...]
    for part in _split3(logf):
        c = c + jnp.dot(tri, part, preferred_element_type=F32)
    carry_ref[...] = c[ts - 1:ts, :]
    parts = jnp.concatenate(list(_split3(c * LOG2E)) + [jnp.ones((ts, LANES), BF16)], axis=1)
    oq_ref[...] = jnp.dot(parts, pq_ref[...], preferred_element_type=F32).astype(BF16)
    ok_ref[...] = jnp.dot(parts, pk_ref[...], preferred_element_type=F32).astype(BF16)


def _fox_aug(fproj, b_f, *, batch, seq, hf, ts):
    T = fproj.shape[0]
    ns = seq // ts
    tri = np.tril(np.ones((ts, ts), np.float32))
    pq = np.zeros((4 * LANES, hf * LANES), np.float32)
    pk = np.zeros((4 * LANES, hf * LANES), np.float32)
    for h in range(hf):
        for p in range(3):
            pq[p * LANES + h, h * LANES + p] = 1.0
            pq[3 * LANES, h * LANES + 3 + p] = 1.0
            pk[3 * LANES, h * LANES + p] = 1.0
            pk[p * LANES + h, h * LANES + 3 + p] = -1.0
    bf = jnp.zeros((1, LANES), F32).at[0, :hf].set(b_f)
    out = jax.ShapeDtypeStruct((T, hf * LANES), BF16)
    return pl.pallas_call(
        functools.partial(_foxaug_kernel, ts=ts),
        grid=(batch, ns),
        in_specs=[
            pl.BlockSpec((ts, LANES), lambda b, s: (b * ns + s, 0)),
            pl.BlockSpec((1, LANES), lambda b, s: (0, 0)),
            pl.BlockSpec((ts, ts), lambda b, s: (0, 0)),
            pl.BlockSpec(pq.shape, lambda b, s: (0, 0)),
            pl.BlockSpec(pk.shape, lambda b, s: (0, 0)),
        ],
        out_specs=[pl.BlockSpec((ts, hf * LANES), lambda b, s: (b * ns + s, 0))] * 2,
        out_shape=[out, out],
        scratch_shapes=[pltpu.VMEM((1, LANES), F32)],
        compiler_params=_params("parallel", "arbitrary"),
        name="fox_aug",
    )(fproj, bf, jnp.asarray(tri, BF16), jnp.asarray(pq, BF16), jnp.asarray(pk, BF16))


def _flash_kernel(*refs, nk, split, hs, tq, tk, mode, has_aug, gate_mode, gate_base, has_addin):
    it = iter(refs)
    q_ref = next(it)
    qa_ref = next(it) if has_aug else None
    k_ref = next(it)
    ka_ref = next(it) if has_aug else None
    v_ref = next(it)
    bias_ref = next(it)
    gate_ref = next(it) if gate_mode else None
    add_ref = next(it) if has_addin else None
    o_ref = next(it)
    scr = list(it)
    per = 1 if mode == "window" else 4
    nch = nk * split
    hc = hs // split
    chains = [scr[c * per:(c + 1) * per] for c in range(nch)]

    qi = pl.program_id(2)
    M = hc * tq
    ones = jnp.ones((tk, LANES), BF16)
    lanes = lambda c: slice((c // split) * LANES, (c // split + 1) * LANES)
    head0 = lambda c: (c // split) * hs + (c % split) * hc

    for c in range(nch):
        q_scr = chains[c][0]
        for h in range(hc):
            col = (head0(c) + h) * LANES
            q_scr[h * tq:(h + 1) * tq, :LANES] = q_ref[:, col:col + LANES]
            if has_aug:
                q_scr[h * tq:(h + 1) * tq, LANES:] = qa_ref[:, lanes(c)]

    def scores(c, k0):
        k = k_ref[pl.ds(k0, tk), lanes(c)]
        if has_aug:
            k = jnp.concatenate([k, ka_ref[pl.ds(k0, tk), lanes(c)]], axis=1)
        return lax.dot_general(chains[c][0][...], k, _NT, preferred_element_type=F32)

    def biased(s, off):
        b = bias_ref[off]
        return (s.reshape(hc, tq, tk) + b[None]).reshape(M, tk) if hc > 1 else s + b

    def values(c, k0):
        return jnp.concatenate([v_ref[pl.ds(k0, tk), lanes(c)], ones], axis=1)

    accs = []
    if mode == "window":
        back = tk - tq
        k0 = pl.multiple_of(jnp.maximum(qi * tq - back, 0), tq)
        for c in range(nch):
            s = biased(scores(c, k0), jnp.minimum(qi, back // tq))
            p = jnp.exp2(s - jnp.max(s, axis=1, keepdims=True))
            accs.append(jnp.dot(p.astype(BF16), values(c, k0), preferred_element_type=F32))
    else:
        for c in range(nch):
            _, m_scr, acc_scr, _ = chains[c]
            m_scr[...] = jnp.full(m_scr.shape, NEG, F32)
            acc_scr[...] = jnp.zeros(acc_scr.shape, F32)

        def stage(c, kt):
            chains[c][3][kt & 1] = scores(c, pl.multiple_of(kt * tk, tk))

        def update(c, kt, off):
            _, m_scr, acc_scr, s_scr = chains[c]
            s = s_scr[kt & 1]
            if off is not None:
                s = biased(s, off)
            m_prev = m_scr[...]
            m_new = jnp.maximum(m_prev, jnp.max(s, axis=1, keepdims=True))
            alpha = jnp.exp2(m_prev - m_new)
            p = jnp.exp2(s - jnp.concatenate([m_new] * (tk // LANES), axis=1))
            pv = jnp.dot(p.astype(BF16), values(c, pl.multiple_of(kt * tk, tk)),
                         preferred_element_type=F32)
            acc_scr[...] = jnp.concatenate([alpha, alpha], axis=1) * acc_scr[...] + pv
            m_scr[...] = m_new

        if mode == "causal":
            last = (qi * tq) // tk
            first, last_off = 0, qi - last * (tk // tq)
        else:
            last = qi
            first, last_off = jnp.maximum(qi - (bias_ref.shape[0] - 1), 0), 0

        for c in range(nch):
            stage(c, first)

        def body(kt, carry):
            for c in range(nch):
                update(c, kt, None if mode == "causal" else qi - kt)
                stage(c, kt + 1)
            return carry

        lax.fori_loop(first, last, body, 0)
        for c in range(nch):
            update(c, last, last_off)
            accs.append(chains[c][2][...])

    for c in range(nch):
        o = accs[c][:, :LANES] * (1.0 / accs[c][:, LANES:])
        for h in range(hc):
            oh = o[h * tq:(h + 1) * tq, :]
            head = head0(c) + h
            sl = slice(head * LANES, (head + 1) * LANES)
            if gate_mode == "elem":
                oh = oh * gate_ref[:, sl].astype(F32)
            elif gate_mode == "col":
                idx = (pl.program_id(1) * hs + head) * 3 + gate_base
                lane = lax.broadcasted_iota(jnp.int32, (tq, LANES), 1)
                oh = oh * jnp.sum(jnp.where(lane == idx, gate_ref[...], 0.0), axis=1, keepdims=True)
            if has_addin:
                oh = oh + add_ref[:, sl]
            o_ref[:, sl] = oh.astype(o_ref.dtype)


def _flash(q_arr, q_col0, k_arr, k_col0, v_arr, v_col0, bias, *, n_kv, hs, tq, tk, mode, out_dtype,
           nk=1, split=1, qaug=None, kaug=None, gate=None, gate_mode=None, gate_col0=0, gate_base=0,
           addin=None):
    B, S, _ = q_arr.shape
    qw = hs * LANES
    assert q_col0 % (nk * qw) == 0 and k_col0 % (nk * LANES) == 0 and v_col0 % (nk * LANES) == 0
    assert S % tq == 0 and S % tk == 0 if mode != "window" else (S % tq == 0 and tk <= S)
    assert tk % tq == 0 and (mode != "band" or tq == tk) and bias.shape[1:] == (tq, tk)
    assert n_kv % nk == 0 and hs % split == 0
    assert nk == 1 or (gate_mode != "col" and (qaug is None or qaug.ndim == 3))
    qb, kb, vb = q_col0 // (nk * qw), k_col0 // (nk * LANES), v_col0 // (nk * LANES)
    assert (qaug is None) == (kaug is None)
    dk = LANES if qaug is None else 2 * LANES
    ins, specs = [q_arr], [pl.BlockSpec((None, tq, nk * qw), lambda b, h, i: (b, i, qb + h))]
    if qaug is not None:
        ins.append(qaug)
        if qaug.ndim == 3:
            specs.append(pl.BlockSpec((None, tq, nk * LANES), lambda b, h, i: (b, i, h)))
        else:
            specs.append(pl.BlockSpec((None, None, tq, LANES), lambda b, h, i: (b, h, i, 0)))
    ins.append(k_arr)
    specs.append(pl.BlockSpec((None, S, nk * LANES), lambda b, h, i: (b, 0, kb + h)))
    if kaug is not None:
        ins.append(kaug)
        if kaug.ndim == 3:
            specs.append(pl.BlockSpec((None, S, nk * LANES), lambda b, h, i: (b, 0, h)))
        else:
            specs.append(pl.BlockSpec((S, LANES), lambda b, h, i: (0, 0)))
    ins.append(v_arr)
    specs.append(pl.BlockSpec((None, S, nk * LANES), lambda b, h, i: (b, 0, vb + h)))
    ins.append(bias)
    specs.append(pl.BlockSpec(bias.shape, lambda b, h, i: (0, 0, 0)))
    if gate_mode == "elem":
        assert gate_col0 % (nk * LANES) == 0
        gb = gate_col0 // (nk * LANES)
        ins.append(gate)
        specs.append(pl.BlockSpec((None, tq, nk * LANES), lambda b, h, i: (b, i, gb + h)))
    elif gate_mode == "col":
        ins.append(gate)
        specs.append(pl.BlockSpec((None, tq, LANES), lambda b, h, i: (b, i, 0)))
    if addin is not None:
        ins.append(addin)
        specs.append(pl.BlockSpec((None, tq, nk * qw), lambda b, h, i: (b, i, h)))
    kern = functools.partial(
        _flash_kernel, nk=nk, split=split, hs=hs, tq=tq, tk=tk, mode=mode, has_aug=qaug is not None,
        gate_mode=gate_mode, gate_base=gate_base, has_addin=addin is not None)
    mc = hs // split * tq
    scratch = []
    for _ in range(nk * split):
        scratch.append(pltpu.VMEM((mc, dk), BF16))
        if mode != "window":
            scratch += [pltpu.VMEM((mc, LANES), F32), pltpu.VMEM((mc, 2 * LANES), F32),
                        pltpu.VMEM((2, mc, tk), F32)]
    return pl.pallas_call(
        kern,
        grid=(B, n_kv // nk, S // tq),
        in_specs=specs,
        out_specs=pl.BlockSpec((None, tq, nk * qw), lambda b, h, i: (b, i, h)),
        out_shape=jax.ShapeDtypeStruct((B, S, n_kv * qw), out_dtype),
        scratch_shapes=scratch,
        compiler_params=_params("parallel", "parallel", "arbitrary"),
        name="flash_attention",
    )(*ins)


def _bias_table(tq, tk, n, weight_of_distance):
    r = np.arange(tq)[:, None]
    c = np.arange(tk)[None, :]
    tabs = []
    for v in range(n):
        w = weight_of_distance(v * tq + r - c)
        tabs.append(np.where(w > 0, np.log2(np.maximum(w, 1.0)), NEG))
    return jnp.asarray(np.stack(tabs), F32)


def _dilated_weight(d):
    w = np.zeros(d.shape, np.float64)
    for window, dil in DILATED_PATTERNS:
        w += (d >= 0) & (d <= window) & (d % dil == 0)
    return w


def _compress_kernel(x_ref, pe_ref, w1_ref, w2_ref, g_ref, tab_ref, o_ref, *, nc):
    half = CMP_STRIDE * HEAD_DIM
    x = x_ref[...].astype(F32)
    top = (x + pe_ref[:, :half]).astype(BF16)
    bot = (x + pe_ref[:, half:]).astype(BF16)
    a = jnp.dot(top, w1_ref[:half, :], preferred_element_type=F32)
    b = jnp.dot(bot, w1_ref[half:, :], preferred_element_type=F32)
    pre = a + pltpu.roll(b, nc - 1, 0)
    hid = pre * (0.5 * (1.0 + jnp.tanh(np.sqrt(2.0 / np.pi) * (pre + 0.044715 * (pre * pre * pre)))))
    y = jnp.dot(hid.astype(BF16), w2_ref[...], preferred_element_type=F32)

    @pl.when(pl.program_id(0) == 0)
    def _():
        ms = jnp.mean(y * y, axis=-1, keepdims=True)
        o_ref[...] = _rope(y * lax.rsqrt(ms + EPS) * g_ref[...], tab_ref).astype(BF16)

    @pl.when(pl.program_id(0) != 0)
    def _():
        o_ref[...] = y.astype(BF16)


def _compress(x2, pe, w1, w2, g_kc, tabs_c):
    _, BG, nc, width = x2.shape
    return pl.pallas_call(
        functools.partial(_compress_kernel, nc=nc),
        grid=(2, BG),
        in_specs=[
            pl.BlockSpec((None, None, nc, width), lambda s, i: (s, i, 0, 0)),
            pl.BlockSpec((None, 1, 2 * width), lambda s, i: (s, 0, 0)),
            pl.BlockSpec((None, 2 * width, HEAD_DIM), lambda s, i: (s, 0, 0)),
            pl.BlockSpec((None, HEAD_DIM, HEAD_DIM), lambda s, i: (s, 0, 0)),
            pl.BlockSpec((1, HEAD_DIM), lambda s, i: (0, 0)),
            pl.BlockSpec((3, nc, LANES), lambda s, i: (0, 0, 0)),
        ],
        out_specs=pl.BlockSpec((None, None, nc, HEAD_DIM), lambda s, i: (s, i, 0, 0)),
        out_shape=jax.ShapeDtypeStruct((2, BG, nc, HEAD_DIM), BF16),
        compiler_params=_params("arbitrary", "arbitrary"),
        name="nsa_compress",
    )(x2, pe, w1, w2, g_kc.reshape(1, HEAD_DIM), tabs_c)


def _cmp_topk_kernel(q_ref, kc_ref, vc_ref, gate_ref, ov_ref, eye_ref, oc_ref, sel_ref, v_scr, *,
                     hs, t, nc, n_slc, n_sel):
    g = pl.program_id(1)
    q0 = pl.program_id(2) * t
    M = hs * t
    q = jnp.concatenate([q_ref[:, h * LANES:(h + 1) * LANES] for h in range(hs)], axis=0)
    s = lax.dot_general(q, kc_ref[...], _NT, preferred_element_type=F32)
    qpos = q0 + lax.broadcasted_iota(jnp.int32, (t, nc), 0)
    n = lax.broadcasted_iota(jnp.int32, (t, nc), 1)
    mask = n * CMP_STRIDE + (CMP_BLOCK - 1) <= qpos
    keep = jnp.where(mask, 1.0, 0.0)[None]
    s = s.reshape(hs, t, nc) + jnp.where(mask, 0.0, NEG)[None]
    m = jnp.max(s, axis=2, keepdims=True)
    e = jnp.exp2(s - m) * keep
    p = (e / jnp.maximum(jnp.sum(e, axis=2, keepdims=True), 1e-30)).reshape(M, nc)
    oc = jnp.dot(p.astype(BF16), vc_ref[...], preferred_element_type=F32)
    lane = lax.broadcasted_iota(jnp.int32, (t, LANES), 1)
    gates = gate_ref[...]
    psum = jnp.zeros((t, nc), F32)
    for h in range(hs):
        idx = (g * hs + h) * 3
        gcol = jnp.sum(jnp.where(lane == idx, gates, 0.0), axis=1, keepdims=True)
        oc_ref[:, h * LANES:(h + 1) * LANES] = oc[h * t:(h + 1) * t, :] * gcol
        psum = psum + p[h * t:(h + 1) * t, :]

    imp = jnp.zeros((LANES, t), F32)
    for part in _split3(psum):
        imp = imp + lax.dot_general(ov_ref[...], part, _NT, preferred_element_type=F32)
    jrow = lax.broadcasted_iota(jnp.int32, (LANES, t), 0)
    cur = (q0 + lax.broadcasted_iota(jnp.int32, (LANES, t), 1)) // SLC_BLOCK
    forced = (jrow == 0) | (jrow == cur) | (jrow == cur - 1)
    val = jnp.where(forced, FORCED_SCORE, imp)
    val = jnp.where(jrow <= cur, val, NEG)
    v_scr[...] = val

    n_grp = -(-n_slc // 8)
    sub8 = lax.broadcasted_iota(jnp.int32, (8, t), 0)
    vals = [val[8 * k:8 * k + 8, :] for k in range(n_grp)]
    ranks = [jnp.zeros((8, t), F32) for _ in range(n_grp)]
    for i in range(n_slc):
        vi = jnp.broadcast_to(v_scr[i:i + 1, :], (8, t))
        for k in range(n_grp):
            ge = jnp.where(vi >= vals[k], 1.0, 0.0)
            gt = jnp.where(vi > vals[k], 1.0, 0.0)
            if 8 * k > i:
                inc = ge
            elif 8 * k + 7 < i:
                inc = gt
            else:
                inc = jnp.where(sub8 > i - 8 * k, ge, gt)
            ranks[k] = ranks[k] + inc
    rank = jnp.concatenate(ranks + [jnp.zeros((LANES - 8 * n_grp, t), F32)], axis=0)
    sel = (rank < n_sel) & (jrow <= cur)
    selb = jnp.where(sel | (jrow >= n_slc), 0.0, NEG).astype(BF16)
    sel_ref[...] = lax.dot_general(eye_ref[...], selb, _NT, preferred_element_type=F32).astype(BF16)


def _cmp_topk(q_arr, kvc, gates, *, batch, seq, hs, t):
    G = N_KV_NSA
    nc = kvc.shape[2]
    n_slc = seq // SLC_BLOCK
    assert n_slc <= LANES and nc == seq // CMP_STRIDE
    qw = hs * LANES
    start = np.arange(nc) * CMP_STRIDE
    js = np.arange(LANES) * SLC_BLOCK
    ov = ((start[None, :] < js[:, None] + SLC_BLOCK) & (start[None, :] + CMP_BLOCK > js[:, None]))
    ov = ov & (np.arange(nc)[None, :] < nc - 1) & (np.arange(LANES)[:, None] < n_slc)
    kern = functools.partial(_cmp_topk_kernel, hs=hs, t=t, nc=nc, n_slc=n_slc,
                             n_sel=min(N_SELECT, n_slc))
    return pl.pallas_call(
        kern,
        grid=(batch, G, seq // t),
        in_specs=[
            pl.BlockSpec((None, t, qw), lambda b, g, i: (b, i, g)),
            pl.BlockSpec((None, None, nc, HEAD_DIM), lambda b, g, i: (0, b * G + g, 0, 0)),
            pl.BlockSpec((None, None, nc, HEAD_DIM), lambda b, g, i: (1, b * G + g, 0, 0)),
            pl.BlockSpec((None, t, LANES), lambda b, g, i: (b, i, 0)),
            pl.BlockSpec((LANES, nc), lambda b, g, i: (0, 0)),
            pl.BlockSpec((t, t), lambda b, g, i: (0, 0)),
        ],
        out_specs=[
            pl.BlockSpec((None, t, qw), lambda b, g, i: (b, i, g)),
            pl.BlockSpec((None, None, t, LANES), lambda b, g, i: (b, g, i, 0)),
        ],
        out_shape=[jax.ShapeDtypeStruct((batch, seq, G * qw), F32),
                   jax.ShapeDtypeStruct((batch, G, seq, LANES), BF16)],
        scratch_shapes=[pltpu.VMEM((LANES, t), F32)],
        compiler_params=_params("parallel", "parallel", "arbitrary"),
        name="nsa_cmp_topk",
    )(q_arr, kvc, kvc, gates, jnp.asarray(ov, BF16), jnp.eye(t, dtype=BF16))


def _rope_tabs(pos):
    half = ROPE_DIM // 2
    inv = 1.0 / (ROPE_THETA ** (jnp.arange(0, ROPE_DIM, 2, dtype=F32) / ROPE_DIM))
    ang = pos.astype(F32)[:, None] * inv
    cos, sin = jnp.cos(ang), jnp.sin(ang)
    n = pos.shape[0]
    c = jnp.ones((n, LANES), F32).at[:, :half].set(cos).at[:, half:ROPE_DIM].set(cos)
    sa = jnp.zeros((n, LANES), F32).at[:, :half].set(-sin)
    sb = jnp.zeros((n, LANES), F32).at[:, half:ROPE_DIM].set(sin)
    return jnp.stack([c, sa, sb])


class Tiles(NamedTuple):
    mlp_rows: int
    proj_rows: int
    fox: int
    dilated: int
    select: int
    window_rows: int
    compressed_rows: int


MLP_FF_TILE = 1024
PROJ_COLS = 1024


def _tiles(seq):
    fit = lambda t: min(t, seq)
    tl = Tiles(mlp_rows=fit(512), proj_rows=fit(1024), fox=fit(512), dilated=fit(512),
               select=fit(512), window_rows=fit(256), compressed_rows=fit(256))
    assert all(seq % t == 0 for t in tl)
    return tl


def _pad_rows(g, rows=8):
    return jnp.zeros((rows, g.shape[-1]), F32).at[:g.shape[0]].set(g)


def _pad_cols(w, n):
    return jnp.zeros((w.shape[0], n), w.dtype).at[:, :w.shape[1]].set(w)


def kernel(x, ln_mix_g, ln_mlp_g, w_mlp_up, w_mlp_down, even_w_in, even_b_f, even_w_out,
           even_g_q_fox, even_g_k_fox, even_g_q_dil, even_g_k_dil, odd_w_in, odd_w_out,
           odd_phi_k_pe, odd_phi_k_w1, odd_phi_k_w2, odd_phi_v_pe, odd_phi_v_w1, odd_phi_v_w2,
           odd_g_q, odd_g_kc, odd_g_ks, odd_g_kw):
    B, S, D = x.shape
    T = B * S
    n_heads = D // HEAD_DIM
    hf = n_heads // 2
    hd = n_heads - hf
    fw, dw = hf * HEAD_DIM, hd * HEAD_DIM
    G = N_KV_NSA
    hpg = n_heads // G
    qw, kvw = n_heads * HEAD_DIM, G * HEAD_DIM
    depth = ln_mix_g.shape[0]

    tl = _tiles(S)
    tm, tm_proj, t_fox, t_band = tl.mlp_rows, tl.proj_rows, tl.fox, tl.window_rows
    tk_sel, t_dil, t_cmp = tl.select, tl.dilated, tl.compressed_rows
    tabs = _rope_tabs(jnp.arange(S))
    nc = S // CMP_STRIDE
    tabs_c = _rope_tabs(jnp.arange(nc) * CMP_STRIDE + CMP_BLOCK - 1)

    causal_w = lambda d: (d >= 0).astype(np.float64)
    win_w = lambda d: ((d >= 0) & (d < WINDOW_NSA)).astype(np.float64)
    dil_span = max(w for w, _ in DILATED_PATTERNS)
    fox_bias = _bias_table(t_fox, t_fox, 1, causal_w)
    dil_bias = _bias_table(t_dil, t_dil, (dil_span + t_dil - 1) // t_dil + 1, _dilated_weight)
    sel_bias = _bias_table(tk_sel, tk_sel, 1, causal_w)
    win_back = -(-(WINDOW_NSA - 1) // t_band) * t_band
    win_bias = _bias_table(t_band, win_back + t_band, win_back // t_band + 1, win_w)
    onehot_blk = jnp.asarray(
        (np.arange(S)[:, None] // SLC_BLOCK == np.arange(LANES)[None, :]).astype(np.float32), BF16)

    h = x.reshape(T, D)
    for layer in range(depth):
        i = layer // 2
        if layer % 2 == 0:
            w_in = even_w_in[i]
            w_main = jnp.concatenate([w_in[:, :4 * fw], w_in[:, 4 * fw + hf:]], axis=1).astype(BF16)
            w_f = _pad_cols(w_in[:, 4 * fw:4 * fw + hf], LANES).astype(BF16)
            gains = _pad_rows(jnp.stack([even_g_q_fox[i], even_g_k_fox[i], even_g_q_dil[i],
                                         even_g_k_dil[i]]))
            segs = [Seg(fw, norm=True, gain=0, scale=QSCALE), Seg(fw, norm=True, gain=1), Seg(fw),
                    Seg(fw, sigmoid=True),
                    Seg(dw, norm=True, gain=2, rope=True, scale=QSCALE),
                    Seg(dw, norm=True, gain=3, rope=True), Seg(dw)]
            proj, fproj = _norm_proj(h, ln_mix_g[layer], w_main, w_f, gains, tabs, segs, seq=S,
                                     tm=tm_proj, tn=min(PROJ_COLS, fw, dw), gate_sigmoid=False)
            proj = proj.reshape(B, S, -1)
            qaug, kaug = _fox_aug(fproj, even_b_f[i], batch=B, seq=S, hf=hf, ts=t_fox)
            o_a = _flash(proj, 0, proj, fw, proj, 2 * fw, fox_bias, n_kv=hf, hs=1, tq=t_fox,
                         tk=t_fox, mode="causal", nk=min(4, hf), out_dtype=BF16,
                         qaug=qaug.reshape(B, S, -1),
                         kaug=kaug.reshape(B, S, -1), gate=proj, gate_mode="elem",
                         gate_col0=3 * fw)
            o_b = _flash(proj, 4 * fw, proj, 4 * fw + dw, proj, 4 * fw + 2 * dw, dil_bias, n_kv=hd,
                         hs=1, tq=t_dil, tk=t_dil, mode="band", nk=min(4, hd), out_dtype=BF16)
            w_out = even_w_out[i].astype(BF16)
            h = _out_proj([o_a.reshape(T, fw), o_b.reshape(T, dw)], [w_out[:fw], w_out[fw:]], h,
                          tm=tm_proj, tn=min(PROJ_COLS, D))
        else:
            w_in = odd_w_in[i]
            n_main = qw + 6 * kvw
            c_kc, c_vc, c_ks, c_kw, c_vs, c_vw = (qw + n * kvw for n in range(6))
            w_main = jnp.concatenate(
                [w_in[:, :qw + 3 * kvw], w_in[:, qw + 4 * kvw:qw + 5 * kvw],
                 w_in[:, qw + 3 * kvw:qw + 4 * kvw], w_in[:, qw + 5 * kvw:n_main]],
                axis=1).astype(BF16)
            w_g = _pad_cols(w_in[:, n_main:], LANES).astype(BF16)
            gains = _pad_rows(jnp.stack([odd_g_q[i], odd_g_ks[i], odd_g_kw[i]]))
            segs = [Seg(qw, norm=True, gain=0, rope=True, scale=QSCALE), Seg(2 * kvw, chunks=True),
                    Seg(2 * kvw, norm=True, gain=1, n_gains=2, rope=True), Seg(2 * kvw)]
            proj, gates, kvraw = _norm_proj(h, ln_mix_g[layer], w_main, w_g, gains, tabs, segs,
                                            seq=S, tm=tm_proj, tn=min(PROJ_COLS, 2 * kvw),
                                            gate_sigmoid=True)
            proj, gates = proj.reshape(B, S, -1), gates.reshape(B, S, LANES)
            x2 = kvraw.reshape(2, B * G, nc, CMP_STRIDE * HEAD_DIM)
            pe = jnp.stack([odd_phi_k_pe[i], odd_phi_v_pe[i]]).reshape(2, 1, CMP_BLOCK * HEAD_DIM)
            w1 = jnp.stack([odd_phi_k_w1[i], odd_phi_v_w1[i]]).astype(BF16)
            w2 = jnp.stack([odd_phi_k_w2[i], odd_phi_v_w2[i]]).astype(BF16)
            kvc = _compress(x2, pe, w1, w2, odd_g_kc[i], tabs_c)
            o_c, selb = _cmp_topk(proj, kvc, gates, batch=B, seq=S, hs=hpg, t=t_cmp)
            o_cs = _flash(proj, 0, proj, c_ks, proj, c_vs, sel_bias, n_kv=G,
                          hs=hpg, tq=tk_sel, tk=tk_sel, mode="causal", split=hpg, out_dtype=F32,
                          qaug=selb,
                          kaug=onehot_blk, gate=gates, gate_mode="col", gate_base=1, addin=o_c)
            o = _flash(proj, 0, proj, c_kw, proj, c_vw, win_bias, n_kv=G, hs=hpg,
                       tq=t_band, tk=win_back + t_band, mode="window", split=hpg, out_dtype=BF16,
                       gate=gates,
                       gate_mode="col", gate_base=2, addin=o_cs)
            h = _out_proj([o.reshape(T, qw)], [odd_w_out[i].astype(BF16)], h, tm=tm_proj, tn=min(PROJ_COLS, D))
        h = _mlp(h, ln_mlp_g[layer], w_mlp_up[layer].astype(BF16), w_mlp_down[layer].astype(BF16),
                 tm=tm, tf=MLP_FF_TILE)
    return h.reshape(B, S, D)
```

```python
import functools
from typing import NamedTuple

import numpy as np
import jax
import jax.numpy as jnp
from jax import lax
from jax.experimental import pallas as pl
from jax.experimental.pallas import tpu as pltpu

HEAD_DIM = 128
LANES = 128
N_KV_NSA = 4
DILATED_PATTERNS = ((128, 1), (512, 4), (2048, 16))
CMP_BLOCK = 32
CMP_STRIDE = 16
SLC_BLOCK = 64
N_SELECT = 16
WINDOW_NSA = 512
ROPE_THETA = 500000.0
ROPE_DIM = HEAD_DIM // 4
EPS = 1e-6
NEG = -1e30
FORCED_SCORE = 1e9
SCALE = HEAD_DIM ** -0.5
LOG2E = float(np.log2(np.e))
QSCALE = SCALE * LOG2E
VMEM_LIMIT = 56 * 1024 * 1024

BF16 = jnp.bfloat16
F32 = jnp.float32
_NT = (((1,), (1,)), ((), ()))


def _params(*sem):
    return pltpu.CompilerParams(dimension_semantics=sem, vmem_limit_bytes=VMEM_LIMIT)


def _split3(x):
    hi = x.astype(BF16)
    r1 = x - hi.astype(F32)
    mid = r1.astype(BF16)
    lo = (r1 - mid.astype(F32)).astype(BF16)
    return hi, mid, lo


class Seg(NamedTuple):
    width: int
    norm: bool = False
    gain: int = 0
    n_gains: int = 1
    rope: bool = False
    scale: float = 1.0
    sigmoid: bool = False
    chunks: bool = False


def _rope(y, tab_ref):
    half = ROPE_DIM // 2
    return (y * tab_ref[0] + pltpu.roll(y, LANES - half, 1) * tab_ref[1]
            + pltpu.roll(y, half, 1) * tab_ref[2])


def _proj_kernel(*refs, tiles, tn, gate_sigmoid):
    x_ref, g_ref, w_ref, wg_ref, gains_ref, tab_ref, o_ref, og_ref = refs[:8]
    oc_ref = refs[8] if any(t.chunks for t in tiles) else None
    xn_ref, acc0, acc1, r_scr = refs[-4:]
    accs = (acc0, acc1)
    j = pl.program_id(1)
    sub = min(tn, 2 * LANES)

    def matmul(t):
        for c in range(tn // sub):
            accs[t % 2][:, c * sub:(c + 1) * sub] = jnp.dot(
                xn_ref[...], w_ref[:, c * sub:(c + 1) * sub], preferred_element_type=F32)

    def epilogue(t):
        seg, acc = tiles[t], accs[t % 2]
        heads = [slice(hd * LANES, (hd + 1) * LANES) for hd in range(tn // LANES)]
        if not seg.norm:
            y = acc[...]
            y = 1.0 / (1.0 + jnp.exp(-y)) if seg.sigmoid else y
            o_ref[...] = y.astype(o_ref.dtype)
            if seg.chunks:
                for hd, sl in enumerate(heads):
                    oc_ref[hd // N_KV_NSA, hd % N_KV_NSA] = y[:, sl].astype(oc_ref.dtype)
            return
        for sl in heads:
            y = acc[:, sl]
            r = lax.rsqrt(jnp.mean(y * y, axis=-1, keepdims=True) + EPS)
            r_scr[:, sl] = jnp.broadcast_to(r, y.shape)
        for hd, sl in enumerate(heads):
            gi = seg.gain + hd // (tn // LANES // seg.n_gains)
            acc[:, sl] = acc[:, sl] * r_scr[:, sl] * gains_ref[gi:gi + 1, :]
        for sl in heads:
            y = _rope(acc[:, sl], tab_ref) if seg.rope else acc[:, sl]
            if seg.scale != 1.0:
                y = y * seg.scale
            o_ref[:, sl] = y.astype(o_ref.dtype)

    @pl.when(j == 0)
    def _():
        x = x_ref[...]
        ms = jnp.mean(x * x, axis=-1, keepdims=True)
        xn_ref[...] = (x * lax.rsqrt(ms + EPS) * g_ref[...]).astype(BF16)
        gate = jnp.dot(xn_ref[...], wg_ref[...], preferred_element_type=F32)
        og_ref[...] = 1.0 / (1.0 + jnp.exp(-gate)) if gate_sigmoid else gate
        matmul(0)

    for t in range(len(tiles)):
        @pl.when(j == t + 1)
        def _(t=t):
            epilogue(t)
            if t + 1 < len(tiles):
                matmul(t + 1)


def _norm_proj(x, g, w, w_gate, gains, tabs, segs, *, seq, tm, tn, gate_sigmoid):
    T, D = x.shape
    N = w.shape[1]
    assert sum(s.width for s in segs) == N and all(s.width % tn == 0 for s in segs)
    assert T % tm == 0 and seq % tm == 0 and tn % LANES == 0
    assert all(s.norm or (not s.rope and s.scale == 1.0) for s in segs)
    assert not any(s.norm and (s.sigmoid or s.chunks) for s in segs)
    assert all(s.width == tn == 2 * N_KV_NSA * LANES for s in segs if s.chunks)
    nseq = seq // tm
    nj = N // tn
    tiles = []
    for s in segs:
        nt = s.width // tn
        assert s.n_gains == 1 or s.n_gains % nt == 0
        for k in range(nt):
            tiles.append(s._replace(width=tn, n_gains=max(s.n_gains // nt, 1),
                                    gain=s.gain + k * (s.n_gains // nt)))
    out_specs = [pl.BlockSpec((tm, tn), lambda i, j: (i, jnp.maximum(j - 1, 0))),
                 pl.BlockSpec((tm, LANES), lambda i, j: (i, 0))]
    out_shape = [jax.ShapeDtypeStruct((T, N), BF16), jax.ShapeDtypeStruct((T, LANES), F32)]
    if any(s.chunks for s in segs):
        out_specs.append(pl.BlockSpec((2, None, N_KV_NSA, tm, LANES),
                                      lambda i, j: (0, i // nseq, 0, i % nseq, 0)))
        out_shape.append(jax.ShapeDtypeStruct((2, T // seq, N_KV_NSA, seq, LANES), BF16))
    return pl.pallas_call(
        functools.partial(_proj_kernel, tiles=tuple(tiles), tn=tn, gate_sigmoid=gate_sigmoid),
        grid=(T // tm, nj + 1),
        in_specs=[
            pl.BlockSpec((tm, D), lambda i, j: (i, 0)),
            pl.BlockSpec((1, D), lambda i, j: (0, 0)),
            pl.BlockSpec((D, tn), lambda i, j: (0, jnp.minimum(j, nj - 1))),
            pl.BlockSpec((D, LANES), lambda i, j: (0, 0)),
            pl.BlockSpec(gains.shape, lambda i, j: (0, 0)),
            pl.BlockSpec((3, tm, LANES), lambda i, j: (0, i % nseq, 0)),
        ],
        out_specs=out_specs,
        out_shape=out_shape,
        scratch_shapes=[pltpu.VMEM((tm, D), BF16), pltpu.VMEM((tm, tn), F32),
                        pltpu.VMEM((tm, tn), F32), pltpu.VMEM((tm, tn), F32)],
        compiler_params=_params("parallel", "arbitrary"),
        name="norm_proj",
    )(x, g.reshape(1, D), w, w_gate, gains, tabs)


def _outproj_kernel(*refs, n_parts):
    a_refs = refs[:n_parts]
    w_refs = refs[n_parts:2 * n_parts]
    h_ref = refs[2 * n_parts]
    o_ref = refs[2 * n_parts + 1]
    acc = h_ref[...]
    for a_ref, w_ref in zip(a_refs, w_refs):
        acc = acc + jnp.dot(a_ref[...], w_ref[...], preferred_element_type=F32)
    o_ref[...] = acc


def _out_proj(parts, weights, h, *, tm, tn):
    T, D = h.shape
    n = len(parts)
    in_specs = [pl.BlockSpec((tm, p.shape[1]), lambda i, j: (i, 0)) for p in parts]
    in_specs += [pl.BlockSpec((w.shape[0], tn), lambda i, j: (0, j)) for w in weights]
    in_specs += [pl.BlockSpec((tm, tn), lambda i, j: (i, j))]
    return pl.pallas_call(
        functools.partial(_outproj_kernel, n_parts=n),
        grid=(T // tm, D // tn),
        in_specs=in_specs,
        out_specs=pl.BlockSpec((tm, tn), lambda i, j: (i, j)),
        out_shape=jax.ShapeDtypeStruct((T, D), F32),
        compiler_params=_params("parallel", "arbitrary"),
        name="out_proj",
    )(*parts, *weights, h)


def _mlp_kernel(h_ref, g_ref, wu_ref, wd_ref, o_ref, xn_ref):
    f = pl.program_id(1)

    @pl.when(f == 0)
    def _():
        x = h_ref[...]
        ms = jnp.mean(x * x, axis=-1, keepdims=True)
        xn_ref[...] = (x * lax.rsqrt(ms + EPS) * g_ref[...]).astype(BF16)
        o_ref[...] = x

    tf = wu_ref.shape[1]
    sub = min(tf, 4 * LANES)
    total = None
    for c in range(tf // sub):
        u = jnp.dot(xn_ref[...], wu_ref[:, c * sub:(c + 1) * sub], preferred_element_type=F32)
        a = jnp.square(jnp.maximum(u, 0.0)).astype(BF16)
        part = jnp.dot(a, wd_ref[c * sub:(c + 1) * sub, :], preferred_element_type=F32)
        total = part if total is None else total + part
    o_ref[...] += total


def _mlp(h, g, w_up, w_down, layer, *, tm, tf):
    T, D = h.shape
    FF = w_up.shape[2]
    return pl.pallas_call(
        _mlp_kernel,
        grid=(T // tm, FF // tf),
        in_specs=[
            pl.BlockSpec((tm, D), lambda i, f: (i, 0)),
            pl.BlockSpec((1, D), lambda i, f: (0, 0)),
            pl.BlockSpec((None, D, tf), lambda i, f: (layer, 0, f)),
            pl.BlockSpec((None, tf, D), lambda i, f: (layer, f, 0)),
        ],
        out_specs=pl.BlockSpec((tm, D), lambda i, f: (i, 0)),
        out_shape=jax.ShapeDtypeStruct((T, D), F32),
        scratch_shapes=[pltpu.VMEM((tm, D), BF16)],
        compiler_params=_params("parallel", "arbitrary"),
        name="sq_relu_mlp",
    )(h, g.reshape(1, D), w_up, w_down)


def _foxaug_kernel(f_ref, bf_ref, tri_ref, pq_ref, pk_ref, oq_ref, ok_ref, carry_ref, *, ts):
    @pl.when(pl.program_id(1) == 0)
    def _():
        carry_ref[...] = jnp.zeros_like(carry_ref)

    x = f_ref[...] + bf_ref[...]
    logf = jnp.minimum(x, 0.0) - jnp.log(1.0 + jnp.exp(-jnp.abs(x)))
    tri = tri_ref[...]
    c = carry_ref[...]
    for part in _split3(logf):
        c = c + jnp.dot(tri, part, preferred_element_type=F32)
    carry_ref[...] = c[ts - 1:ts, :]
    parts = jnp.concatenate(list(_split3(c * LOG2E)) + [jnp.ones((ts, LANES), BF16)], axis=1)
    oq_ref[...] = jnp.dot(parts, pq_ref[...], preferred_element_type=F32).astype(BF16)
    ok_ref[...] = jnp.dot(parts, pk_ref[...], preferred_element_type=F32).astype(BF16)


def _fox_aug(fproj, b_f, *, batch, seq, hf, ts):
    T = fproj.shape[0]
    ns = seq // ts
    tri = np.tril(np.ones((ts, ts), np.float32))
    pq = np.zeros((4 * LANES, hf * LANES), np.float32)
    pk = np.zeros((4 * LANES, hf * LANES), np.float32)
    for h in range(hf):
        for p in range(3):
            pq[p * LANES + h, h * LANES + p] = 1.0
            pq[3 * LANES, h * LANES + 3 + p] = 1.0
            pk[3 * LANES, h * LANES + p] = 1.0
            pk[p * LANES + h, h * LANES + 3 + p] = -1.0
    bf = jnp.zeros((1, LANES), F32).at[0, :hf].set(b_f)
    out = jax.ShapeDtypeStruct((T, hf * LANES), BF16)
    return pl.pallas_call(
        functools.partial(_foxaug_kernel, ts=ts),
        grid=(batch, ns),
        in_specs=[
            pl.BlockSpec((ts, LANES), lambda b, s: (b * ns + s, 0)),
            pl.BlockSpec((1, LANES), lambda b, s: (0, 0)),
            pl.BlockSpec((ts, ts), lambda b, s: (0, 0)),
            pl.BlockSpec(pq.shape, lambda b, s: (0, 0)),
            pl.BlockSpec(pk.shape, lambda b, s: (0, 0)),
        ],
        out_specs=[pl.BlockSpec((ts, hf * LANES), lambda b, s: (b * ns + s, 0))] * 2,
        out_shape=[out, out],
        scratch_shapes=[pltpu.VMEM((1, LANES), F32)],
        compiler_params=_params("parallel", "arbitrary"),
        name="fox_aug",
    )(fproj, bf, jnp.asarray(tri, BF16), jnp.asarray(pq, BF16), jnp.asarray(pk, BF16))


def _flash_kernel(*refs, nk, split, hs, tq, tk, mode, has_aug, gate_mode, gate_base, has_addin):
    it = iter(refs)
    q_ref = next(it)
    qa_ref = next(it) if has_aug else None
    k_ref = next(it)
    ka_ref = next(it) if has_aug else None
    v_ref = next(it)
    bias_ref = next(it)
    gate_ref = next(it) if gate_mode else None
    add_ref = next(it) if has_addin else None
    o_ref = next(it)
    scr = list(it)
    per = 1 if mode == "window" else 4
    nch = nk * split
    hc = hs // split
    chains = [scr[c * per:(c + 1) * per] for c in range(nch)]

    qi = pl.program_id(2)
    M = hc * tq
    ones = jnp.ones((tk, LANES), BF16)
    lanes = lambda c: slice((c // split) * LANES, (c // split + 1) * LANES)
    head0 = lambda c: (c // split) * hs + (c % split) * hc

    for c in range(nch):
        q_scr = chains[c][0]
        for h in range(hc):
            col = (head0(c) + h) * LANES
            q_scr[h * tq:(h + 1) * tq, :LANES] = q_ref[:, col:col + LANES]
            if has_aug:
                q_scr[h * tq:(h + 1) * tq, LANES:] = qa_ref[:, lanes(c)]

    def scores(c, k0):
        k = k_ref[pl.ds(k0, tk), lanes(c)]
        if has_aug:
            k = jnp.concatenate([k, ka_ref[pl.ds(k0, tk), lanes(c)]], axis=1)
        return lax.dot_general(chains[c][0][...], k, _NT, preferred_element_type=F32)

    def biased(s, off):
        b = bias_ref[off]
        return (s.reshape(hc, tq, tk) + b[None]).reshape(M, tk) if hc > 1 else s + b

    def values(c, k0):
        return jnp.concatenate([v_ref[pl.ds(k0, tk), lanes(c)], ones], axis=1)

    accs = []
    if mode == "window":
        back = tk - tq
        k0 = pl.multiple_of(jnp.maximum(qi * tq - back, 0), tq)
        for c in range(nch):
            s = biased(scores(c, k0), jnp.minimum(qi, back // tq))
            p = jnp.exp2(s - jnp.max(s, axis=1, keepdims=True))
            accs.append(jnp.dot(p.astype(BF16), values(c, k0), preferred_element_type=F32))
    else:
        for c in range(nch):
            _, m_scr, acc_scr, _ = chains[c]
            m_scr[...] = jnp.full(m_scr.shape, NEG, F32)
            acc_scr[...] = jnp.zeros(acc_scr.shape, F32)

        def stage(c, kt):
            chains[c][3][kt & 1] = scores(c, pl.multiple_of(kt * tk, tk))

        def update(c, kt, off):
            _, m_scr, acc_scr, s_scr = chains[c]
            s = s_scr[kt & 1]
            if off is not None:
                s = biased(s, off)
            m_prev = m_scr[...]
            m_new = jnp.maximum(m_prev, jnp.max(s, axis=1, keepdims=True))
            alpha = jnp.exp2(m_prev - m_new)
            p = jnp.exp2(s - jnp.concatenate([m_new] * (tk // LANES), axis=1))
            pv = jnp.dot(p.astype(BF16), values(c, pl.multiple_of(kt * tk, tk)),
                         preferred_element_type=F32)
            acc_scr[...] = jnp.concatenate([alpha, alpha], axis=1) * acc_scr[...] + pv
            m_scr[...] = m_new

        if mode == "causal":
            last = (qi * tq) // tk
            first, last_off = 0, qi - last * (tk // tq)
        else:
            last = qi
            first, last_off = jnp.maximum(qi - (bias_ref.shape[0] - 1), 0), 0

        for c in range(nch):
            stage(c, first)

        def body(kt, carry):
            for c in range(nch):
                update(c, kt, None if mode == "causal" else qi - kt)
                stage(c, kt + 1)
            return carry

        lax.fori_loop(first, last, body, 0)
        for c in range(nch):
            update(c, last, last_off)
            accs.append(chains[c][2][...])

    for c in range(nch):
        o = accs[c][:, :LANES] * (1.0 / accs[c][:, LANES:])
        for h in range(hc):
            oh = o[h * tq:(h + 1) * tq, :]
            head = head0(c) + h
            sl = slice(head * LANES, (head + 1) * LANES)
            if gate_mode == "elem":
                oh = oh * gate_ref[:, sl].astype(F32)
            elif gate_mode == "col":
                idx = (pl.program_id(1) * hs + head) * 3 + gate_base
                lane = lax.broadcasted_iota(jnp.int32, (tq, LANES), 1)
                oh = oh * jnp.sum(jnp.where(lane == idx, gate_ref[...], 0.0), axis=1, keepdims=True)
            if has_addin:
                oh = oh + add_ref[:, sl]
            o_ref[:, sl] = oh.astype(o_ref.dtype)


def _flash(q_arr, q_col0, k_arr, k_col0, v_arr, v_col0, bias, *, n_kv, hs, tq, tk, mode, out_dtype,
           nk=1, split=1, qaug=None, kaug=None, gate=None, gate_mode=None, gate_col0=0, gate_base=0,
           addin=None):
    B, S, _ = q_arr.shape
    qw = hs * LANES
    assert q_col0 % (nk * qw) == 0 and k_col0 % (nk * LANES) == 0 and v_col0 % (nk * LANES) == 0
    assert S % tq == 0 and S % tk == 0 if mode != "window" else (S % tq == 0 and tk <= S)
    assert tk % tq == 0 and (mode != "band" or tq == tk) and bias.shape[1:] == (tq, tk)
    assert n_kv % nk == 0 and hs % split == 0
    assert nk == 1 or (gate_mode != "col" and (qaug is None or qaug.ndim == 3))
    qb, kb, vb = q_col0 // (nk * qw), k_col0 // (nk * LANES), v_col0 // (nk * LANES)
    assert (qaug is None) == (kaug is None)
    dk = LANES if qaug is None else 2 * LANES
    ins, specs = [q_arr], [pl.BlockSpec((None, tq, nk * qw), lambda b, h, i: (b, i, qb + h))]
    if qaug is not None:
        ins.append(qaug)
        if qaug.ndim == 3:
            specs.append(pl.BlockSpec((None, tq, nk * LANES), lambda b, h, i: (b, i, h)))
        else:
            specs.append(pl.BlockSpec((None, None, tq, LANES), lambda b, h, i: (b, h, i, 0)))
    ins.append(k_arr)
    specs.append(pl.BlockSpec((None, S, nk * LANES), lambda b, h, i: (b, 0, kb + h)))
    if kaug is not None:
        ins.append(kaug)
        if kaug.ndim == 3:
            specs.append(pl.BlockSpec((None, S, nk * LANES), lambda b, h, i: (b, 0, h)))
        else:
            specs.append(pl.BlockSpec((S, LANES), lambda b, h, i: (0, 0)))
    ins.append(v_arr)
    specs.append(pl.BlockSpec((None, S, nk * LANES), lambda b, h, i: (b, 0, vb + h)))
    ins.append(bias)
    specs.append(pl.BlockSpec(bias.shape, lambda b, h, i: (0, 0, 0)))
    if gate_mode == "elem":
        assert gate_col0 % (nk * LANES) == 0
        gb = gate_col0 // (nk * LANES)
        ins.append(gate)
        specs.append(pl.BlockSpec((None, tq, nk * LANES), lambda b, h, i: (b, i, gb + h)))
    elif gate_mode == "col":
        ins.append(gate)
        specs.append(pl.BlockSpec((None, tq, LANES), lambda b, h, i: (b, i, 0)))
    if addin is not None:
        ins.append(addin)
        specs.append(pl.BlockSpec((None, tq, nk * qw), lambda b, h, i: (b, i, h)))
    kern = functools.partial(
        _flash_kernel, nk=nk, split=split, hs=hs, tq=tq, tk=tk, mode=mode, has_aug=qaug is not None,
        gate_mode=gate_mode, gate_base=gate_base, has_addin=addin is not None)
    mc = hs // split * tq
    scratch = []
    for _ in range(nk * split):
        scratch.append(pltpu.VMEM((mc, dk), BF16))
        if mode != "window":
            scratch += [pltpu.VMEM((mc, LANES), F32), pltpu.VMEM((mc, 2 * LANES), F32),
                        pltpu.VMEM((2, mc, tk), F32)]
    return pl.pallas_call(
        kern,
        grid=(B, n_kv // nk, S // tq),
        in_specs=specs,
        out_specs=pl.BlockSpec((None, tq, nk * qw), lambda b, h, i: (b, i, h)),
        out_shape=jax.ShapeDtypeStruct((B, S, n_kv * qw), out_dtype),
        scratch_shapes=scratch,
        compiler_params=_params("parallel", "parallel", "arbitrary"),
        name="flash_attention",
    )(*ins)


def _bias_table(tq, tk, n, weight_of_distance):
    r = np.arange(tq)[:, None]
    c = np.arange(tk)[None, :]
    tabs = []
    for v in range(n):
        w = weight_of_distance(v * tq + r - c)
        tabs.append(np.where(w > 0, np.log2(np.maximum(w, 1.0)), NEG))
    return jnp.asarray(np.stack(tabs), F32)


def _dilated_weight(d):
    w = np.zeros(d.shape, np.float64)
    for window, dil in DILATED_PATTERNS:
        w += (d >= 0) & (d <= window) & (d % dil == 0)
    return w


def _compress_kernel(x_ref, pe_ref, w1_ref, w2_ref, g_ref, tab_ref, o_ref, *, nc):
    half = CMP_STRIDE * HEAD_DIM
    x = x_ref[...].astype(F32)
    top = (x + pe_ref[:, :half]).astype(BF16)
    bot = (x + pe_ref[:, half:]).astype(BF16)
    a = jnp.dot(top, w1_ref[:half, :], preferred_element_type=F32)
    b = jnp.dot(bot, w1_ref[half:, :], preferred_element_type=F32)
    pre = a + pltpu.roll(b, nc - 1, 0)
    hid = pre * (0.5 * (1.0 + jnp.tanh(np.sqrt(2.0 / np.pi) * (pre + 0.044715 * (pre * pre * pre)))))
    y = jnp.dot(hid.astype(BF16), w2_ref[...], preferred_element_type=F32)

    @pl.when(pl.program_id(0) == 0)
    def _():
        ms = jnp.mean(y * y, axis=-1, keepdims=True)
        o_ref[...] = _rope(y * lax.rsqrt(ms + EPS) * g_ref[...], tab_ref).astype(BF16)

    @pl.when(pl.program_id(0) != 0)
    def _():
        o_ref[...] = y.astype(BF16)


def _compress(x2, pe, w1, w2, g_kc, tabs_c):
    _, BG, nc, width = x2.shape
    return pl.pallas_call(
        functools.partial(_compress_kernel, nc=nc),
        grid=(2, BG),
        in_specs=[
            pl.BlockSpec((None, None, nc, width), lambda s, i: (s, i, 0, 0)),
            pl.BlockSpec((None, 1, 2 * width), lambda s, i: (s, 0, 0)),
            pl.BlockSpec((None, 2 * width, HEAD_DIM), lambda s, i: (s, 0, 0)),
            pl.BlockSpec((None, HEAD_DIM, HEAD_DIM), lambda s, i: (s, 0, 0)),
            pl.BlockSpec((1, HEAD_DIM), lambda s, i: (0, 0)),
            pl.BlockSpec((3, nc, LANES), lambda s, i: (0, 0, 0)),
        ],
        out_specs=pl.BlockSpec((None, None, nc, HEAD_DIM), lambda s, i: (s, i, 0, 0)),
        out_shape=jax.ShapeDtypeStruct((2, BG, nc, HEAD_DIM), BF16),
        compiler_params=_params("arbitrary", "arbitrary"),
        name="nsa_compress",
    )(x2, pe, w1, w2, g_kc.reshape(1, HEAD_DIM), tabs_c)


def _cmp_topk_kernel(q_ref, kc_ref, vc_ref, gate_ref, ov_ref, eye_ref, oc_ref, sel_ref, v_scr, *,
                     hs, t, nc, n_slc, n_sel):
    g = pl.program_id(1)
    q0 = pl.program_id(2) * t
    M = hs * t
    q = jnp.concatenate([q_ref[:, h * LANES:(h + 1) * LANES] for h in range(hs)], axis=0)
    s = lax.dot_general(q, kc_ref[...], _NT, preferred_element_type=F32)
    qpos = q0 + lax.broadcasted_iota(jnp.int32, (t, nc), 0)
    n = lax.broadcasted_iota(jnp.int32, (t, nc), 1)
    mask = n * CMP_STRIDE + (CMP_BLOCK - 1) <= qpos
    keep = jnp.where(mask, 1.0, 0.0)[None]
    s = s.reshape(hs, t, nc) + jnp.where(mask, 0.0, NEG)[None]
    m = jnp.max(s, axis=2, keepdims=True)
    e = jnp.exp2(s - m) * keep
    p = (e / jnp.maximum(jnp.sum(e, axis=2, keepdims=True), 1e-30)).reshape(M, nc)
    oc = jnp.dot(p.astype(BF16), vc_ref[...], preferred_element_type=F32)
    lane = lax.broadcasted_iota(jnp.int32, (t, LANES), 1)
    gates = gate_ref[...]
    psum = jnp.zeros((t, nc), F32)
    for h in range(hs):
        idx = (g * hs + h) * 3
        gcol = jnp.sum(jnp.where(lane == idx, gates, 0.0), axis=1, keepdims=True)
        oc_ref[:, h * LANES:(h + 1) * LANES] = oc[h * t:(h + 1) * t, :] * gcol
        psum = psum + p[h * t:(h + 1) * t, :]

    imp = jnp.zeros((LANES, t), F32)
    for part in _split3(psum):
        imp = imp + lax.dot_general(ov_ref[...], part, _NT, preferred_element_type=F32)
    jrow = lax.broadcasted_iota(jnp.int32, (LANES, t), 0)
    cur = (q0 + lax.broadcasted_iota(jnp.int32, (LANES, t), 1)) // SLC_BLOCK
    forced = (jrow == 0) | (jrow == cur) | (jrow == cur - 1)
    val = jnp.where(forced, FORCED_SCORE, imp)
    val = jnp.where(jrow <= cur, val, NEG)
    v_scr[...] = val

    n_grp = -(-n_slc // 8)
    sub8 = lax.broadcasted_iota(jnp.int32, (8, t), 0)
    vals = [val[8 * k:8 * k + 8, :] for k in range(n_grp)]
    ranks = [jnp.zeros((8, t), F32) for _ in range(n_grp)]
    for i in range(n_slc):
        vi = jnp.broadcast_to(v_scr[i:i + 1, :], (8, t))
        for k in range(n_grp):
            ge = jnp.where(vi >= vals[k], 1.0, 0.0)
            gt = jnp.where(vi > vals[k], 1.0, 0.0)
            if 8 * k > i:
                inc = ge
            elif 8 * k + 7 < i:
                inc = gt
            else:
                inc = jnp.where(sub8 > i - 8 * k, ge, gt)
            ranks[k] = ranks[k] + inc
    rank = jnp.concatenate(ranks + [jnp.zeros((LANES - 8 * n_grp, t), F32)], axis=0)
    sel = (rank < n_sel) & (jrow <= cur)
    selb = jnp.where(sel | (jrow >= n_slc), 0.0, NEG).astype(BF16)
    sel_ref[...] = lax.dot_general(eye_ref[...], selb, _NT, preferred_element_type=F32).astype(BF16)


def _cmp_topk(q_arr, kvc, gates, *, batch, seq, hs, t):
    G = N_KV_NSA
    nc = kvc.shape[2]
    n_slc = seq // SLC_BLOCK
    assert n_slc <= LANES and nc == seq // CMP_STRIDE
    qw = hs * LANES
    start = np.arange(nc) * CMP_STRIDE
    js = np.arange(LANES) * SLC_BLOCK
    ov = ((start[None, :] < js[:, None] + SLC_BLOCK) & (start[None, :] + CMP_BLOCK > js[:, None]))
    ov = ov & (np.arange(nc)[None, :] < nc - 1) & (np.arange(LANES)[:, None] < n_slc)
    kern = functools.partial(_cmp_topk_kernel, hs=hs, t=t, nc=nc, n_slc=n_slc,
                             n_sel=min(N_SELECT, n_slc))
    return pl.pallas_call(
        kern,
        grid=(batch, G, seq // t),
        in_specs=[
            pl.BlockSpec((None, t, qw), lambda b, g, i: (b, i, g)),
            pl.BlockSpec((None, None, nc, HEAD_DIM), lambda b, g, i: (0, b * G + g, 0, 0)),
            pl.BlockSpec((None, None, nc, HEAD_DIM), lambda b, g, i: (1, b * G + g, 0, 0)),
            pl.BlockSpec((None, t, LANES), lambda b, g, i: (b, i, 0)),
            pl.BlockSpec((LANES, nc), lambda b, g, i: (0, 0)),
            pl.BlockSpec((t, t), lambda b, g, i: (0, 0)),
        ],
        out_specs=[
            pl.BlockSpec((None, t, qw), lambda b, g, i: (b, i, g)),
            pl.BlockSpec((None, None, t, LANES), lambda b, g, i: (b, g, i, 0)),
        ],
        out_shape=[jax.ShapeDtypeStruct((batch, seq, G * qw), F32),
                   jax.ShapeDtypeStruct((batch, G, seq, LANES), BF16)],
        scratch_shapes=[pltpu.VMEM((LANES, t), F32)],
        compiler_params=_params("parallel", "parallel", "arbitrary"),
        name="nsa_cmp_topk",
    )(q_arr, kvc, kvc, gates, jnp.asarray(ov, BF16), jnp.eye(t, dtype=BF16))


def _rope_tabs(pos):
    half = ROPE_DIM // 2
    inv = 1.0 / (ROPE_THETA ** (jnp.arange(0, ROPE_DIM, 2, dtype=F32) / ROPE_DIM))
    ang = pos.astype(F32)[:, None] * inv
    cos, sin = jnp.cos(ang), jnp.sin(ang)
    n = pos.shape[0]
    c = jnp.ones((n, LANES), F32).at[:, :half].set(cos).at[:, half:ROPE_DIM].set(cos)
    sa = jnp.zeros((n, LANES), F32).at[:, :half].set(-sin)
    sb = jnp.zeros((n, LANES), F32).at[:, half:ROPE_DIM].set(sin)
    return jnp.stack([c, sa, sb])


class Tiles(NamedTuple):
    mlp_rows: int
    proj_rows: int
    fox: int
    dilated: int
    select: int
    window_rows: int
    compressed_rows: int


MLP_FF_TILE = 1024
PROJ_COLS = 1024


def _tiles(seq):
    fit = lambda t: min(t, seq)
    tl = Tiles(mlp_rows=fit(512), proj_rows=fit(1024), fox=fit(512), dilated=fit(512),
               select=fit(512), window_rows=fit(256), compressed_rows=fit(256))
    assert all(seq % t == 0 for t in tl)
    return tl


def _pad_rows(g, rows=8):
    return jnp.zeros((rows, g.shape[-1]), F32).at[:g.shape[0]].set(g)


def _pad_cols(w, n):
    return jnp.zeros((w.shape[0], n), w.dtype).at[:, :w.shape[1]].set(w)


def kernel(x, ln_mix_g, ln_mlp_g, w_mlp_up, w_mlp_down, even_w_in, even_b_f, even_w_out,
           even_g_q_fox, even_g_k_fox, even_g_q_dil, even_g_k_dil, odd_w_in, odd_w_out,
           odd_phi_k_pe, odd_phi_k_w1, odd_phi_k_w2, odd_phi_v_pe, odd_phi_v_w1, odd_phi_v_w2,
           odd_g_q, odd_g_kc, odd_g_ks, odd_g_kw):
    B, S, D = x.shape
    T = B * S
    n_heads = D // HEAD_DIM
    hf = n_heads // 2
    hd = n_heads - hf
    fw, dw = hf * HEAD_DIM, hd * HEAD_DIM
    G = N_KV_NSA
    hpg = n_heads // G
    qw, kvw = n_heads * HEAD_DIM, G * HEAD_DIM
    depth = ln_mix_g.shape[0]

    tl = _tiles(S)
    tm, tm_proj, t_fox, t_band = tl.mlp_rows, tl.proj_rows, tl.fox, tl.window_rows
    tk_sel, t_dil, t_cmp = tl.select, tl.dilated, tl.compressed_rows
    tabs = _rope_tabs(jnp.arange(S))
    nc = S // CMP_STRIDE
    tabs_c = _rope_tabs(jnp.arange(nc) * CMP_STRIDE + CMP_BLOCK - 1)

    causal_w = lambda d: (d >= 0).astype(np.float64)
    win_w = lambda d: ((d >= 0) & (d < WINDOW_NSA)).astype(np.float64)
    dil_span = max(w for w, _ in DILATED_PATTERNS)
    fox_bias = _bias_table(t_fox, t_fox, 1, causal_w)
    dil_bias = _bias_table(t_dil, t_dil, (dil_span + t_dil - 1) // t_dil + 1, _dilated_weight)
    sel_bias = _bias_table(tk_sel, tk_sel, 1, causal_w)
    win_back = -(-(WINDOW_NSA - 1) // t_band) * t_band
    win_bias = _bias_table(t_band, win_back + t_band, win_back // t_band + 1, win_w)
    onehot_blk = jnp.asarray(
        (np.arange(S)[:, None] // SLC_BLOCK == np.arange(LANES)[None, :]).astype(np.float32), BF16)

    w_up_bf16, w_down_bf16 = w_mlp_up.astype(BF16), w_mlp_down.astype(BF16)
    h = x.reshape(T, D)
    for layer in range(depth):
        i = layer // 2
        if layer % 2 == 0:
            w_in = even_w_in[i]
            w_main = jnp.concatenate([w_in[:, :4 * fw], w_in[:, 4 * fw + hf:]], axis=1).astype(BF16)
            w_f = _pad_cols(w_in[:, 4 * fw:4 * fw + hf], LANES).astype(BF16)
            gains = _pad_rows(jnp.stack([even_g_q_fox[i], even_g_k_fox[i], even_g_q_dil[i],
                                         even_g_k_dil[i]]))
            segs = [Seg(fw, norm=True, gain=0, scale=QSCALE), Seg(fw, norm=True, gain=1), Seg(fw),
                    Seg(fw, sigmoid=True),
                    Seg(dw, norm=True, gain=2, rope=True, scale=QSCALE),
                    Seg(dw, norm=True, gain=3, rope=True), Seg(dw)]
            proj, fproj = _norm_proj(h, ln_mix_g[layer], w_main, w_f, gains, tabs, segs, seq=S,
                                     tm=tm_proj, tn=min(PROJ_COLS, fw, dw), gate_sigmoid=False)
            proj = proj.reshape(B, S, -1)
            qaug, kaug = _fox_aug(fproj, even_b_f[i], batch=B, seq=S, hf=hf, ts=t_fox)
            o_a = _flash(proj, 0, proj, fw, proj, 2 * fw, fox_bias, n_kv=hf, hs=1, tq=t_fox,
                         tk=t_fox, mode="causal", nk=min(4, hf), out_dtype=BF16,
                         qaug=qaug.reshape(B, S, -1),
                         kaug=kaug.reshape(B, S, -1), gate=proj, gate_mode="elem",
                         gate_col0=3 * fw)
            o_b = _flash(proj, 4 * fw, proj, 4 * fw + dw, proj, 4 * fw + 2 * dw, dil_bias, n_kv=hd,
                         hs=1, tq=t_dil, tk=t_dil, mode="band", nk=min(4, hd), out_dtype=BF16)
            w_out = even_w_out[i].astype(BF16)
            h = _out_proj([o_a.reshape(T, fw), o_b.reshape(T, dw)], [w_out[:fw], w_out[fw:]], h,
                          tm=tm_proj, tn=min(PROJ_COLS, D))
        else:
            w_in = odd_w_in[i]
            n_main = qw + 6 * kvw
            c_kc, c_vc, c_ks, c_kw, c_vs, c_vw = (qw + n * kvw for n in range(6))
            w_main = jnp.concatenate(
                [w_in[:, :qw + 3 * kvw], w_in[:, qw + 4 * kvw:qw + 5 * kvw],
                 w_in[:, qw + 3 * kvw:qw + 4 * kvw], w_in[:, qw + 5 * kvw:n_main]],
                axis=1).astype(BF16)
            w_g = _pad_cols(w_in[:, n_main:], LANES).astype(BF16)
            gains = _pad_rows(jnp.stack([odd_g_q[i], odd_g_ks[i], odd_g_kw[i]]))
            segs = [Seg(qw, norm=True, gain=0, rope=True, scale=QSCALE), Seg(2 * kvw, chunks=True),
                    Seg(2 * kvw, norm=True, gain=1, n_gains=2, rope=True), Seg(2 * kvw)]
            proj, gates, kvraw = _norm_proj(h, ln_mix_g[layer], w_main, w_g, gains, tabs, segs,
                                            seq=S, tm=tm_proj, tn=min(PROJ_COLS, 2 * kvw),
                                            gate_sigmoid=True)
            proj, gates = proj.reshape(B, S, -1), gates.reshape(B, S, LANES)
            x2 = kvraw.reshape(2, B * G, nc, CMP_STRIDE * HEAD_DIM)
            pe = jnp.stack([odd_phi_k_pe[i], odd_phi_v_pe[i]]).reshape(2, 1, CMP_BLOCK * HEAD_DIM)
            w1 = jnp.stack([odd_phi_k_w1[i], odd_phi_v_w1[i]]).astype(BF16)
            w2 = jnp.stack([odd_phi_k_w2[i], odd_phi_v_w2[i]]).astype(BF16)
            kvc = _compress(x2, pe, w1, w2, odd_g_kc[i], tabs_c)
            o_c, selb = _cmp_topk(proj, kvc, gates, batch=B, seq=S, hs=hpg, t=t_cmp)
            o_cs = _flash(proj, 0, proj, c_ks, proj, c_vs, sel_bias, n_kv=G,
                          hs=hpg, tq=tk_sel, tk=tk_sel, mode="causal", split=hpg, out_dtype=F32,
                          qaug=selb,
                          kaug=onehot_blk, gate=gates, gate_mode="col", gate_base=1, addin=o_c)
            o = _flash(proj, 0, proj, c_kw, proj, c_vw, win_bias, n_kv=G, hs=hpg,
                       tq=t_band, tk=win_back + t_band, mode="window", split=hpg, out_dtype=BF16,
                       gate=gates,
                       gate_mode="col", gate_base=2, addin=o_cs)
            h = _out_proj([o.reshape(T, qw)], [odd_w_out[i].astype(BF16)], h, tm=tm_proj, tn=min(PROJ_COLS, D))
        h = _mlp(h, ln_mlp_g[layer], w_up_bf16, w_down_bf16, layer, tm=tm, tf=MLP_FF_TILE)
    return h.reshape(B, S, D)
```
